```python
import math
import jax
import jax.numpy as jnp
from jax import lax
import numpy as np

D_MODEL = 4096
BATCH = 2
SEQ = 8192
DEPTH = 2

D_MIX = D_MODEL
N_MIXERS = 4
D_GROUP = D_MIX // N_MIXERS

SC_WIDTH = 3
CF_WIDTH = 31
GDN_HEAD_DIM = 128
GDN_HEADS = D_GROUP // GDN_HEAD_DIM
GDN_CONV = 4
GDN_CHUNK = 64
POOL_WINDOWS = (2, 4, 8, 16)
POOL_GROUP = D_GROUP // len(POOL_WINDOWS)
D_FF = -(-8 * D_MODEL // (3 * 256)) * 256

NORM_EPS = 1e-6

IN_SIZES = (
    D_GROUP, D_GROUP, D_GROUP,
    2 * D_GROUP,
    D_GROUP, D_GROUP, D_GROUP, D_GROUP,
    GDN_HEADS, GDN_HEADS,
    D_GROUP,
)
IN_COLS = sum(IN_SIZES)

kernel_name = "hybrid_parallel_conv_deltanet_pool"


def rms_norm(x, w):
    xf = x.astype(jnp.float32)
    y = xf * lax.rsqrt(jnp.mean(xf * xf, axis=-1, keepdims=True) + NORM_EPS)
    return (y * w.astype(jnp.float32)).astype(x.dtype)


def layer_norm(x, w, b):
    xf = x.astype(jnp.float32)
    mu = jnp.mean(xf, axis=-1, keepdims=True)
    xc = xf - mu
    y = xc * lax.rsqrt(jnp.mean(xc * xc, axis=-1, keepdims=True) + NORM_EPS)
    return (y * w.astype(jnp.float32) + b.astype(jnp.float32)).astype(x.dtype)


def l2norm(x):
    xf = x.astype(jnp.float32)
    return xf * lax.rsqrt(jnp.sum(xf * xf, axis=-1, keepdims=True) + NORM_EPS)


def causal_dwconv(x, w):
    k, c = w.shape
    return lax.conv_general_dilated(
        x, w[:, None, :].astype(x.dtype), window_strides=(1,), padding=[(k - 1, 0)],
        dimension_numbers=("NWC", "WIO", "NWC"), feature_group_count=c)


def split_cols(h):
    outs, start = [], 0
    for size in IN_SIZES:
        outs.append(h[..., start:start + size])
        start += size
    return outs


def gated_delta_rule(q, k, v, g, beta):
    b, s, nh, dk = q.shape
    dv = v.shape[-1]
    c = GDN_CHUNK
    n = s // c

    def blocks(t):
        t = t.reshape((b, n, c, nh) + t.shape[3:])
        return jnp.moveaxis(t, 3, 1)

    q = blocks(q * (dk ** -0.5))
    k = blocks(k)
    v = blocks(v)
    g = blocks(g)
    beta = blocks(beta)
    gc = jnp.cumsum(g, axis=-1)
    causal = jnp.tril(jnp.ones((c, c), dtype=bool))
    strict = jnp.tril(jnp.ones((c, c), dtype=bool), k=-1)
    decay = jnp.exp(jnp.where(causal, gc[..., :, None] - gc[..., None, :], -jnp.inf))
    kb = k * beta[..., None]
    vb = v * beta[..., None]
    m = jnp.where(strict, jnp.einsum("bhnik,bhnjk->bhnij", kb, k) * decay, 0.0)
    eye = jnp.eye(c, dtype=q.dtype)
    t_inv = lax.linalg.triangular_solve(eye + m, jnp.broadcast_to(eye, m.shape),
                                        left_side=True, lower=True)
    u = jnp.einsum("bhnij,bhnjv->bhniv", t_inv, vb)
    w = jnp.einsum("bhnij,bhnjk->bhnik", t_inv, kb * jnp.exp(gc)[..., None])
    a_intra = jnp.einsum("bhnik,bhnjk->bhnij", q, k) * decay
    q_dec = q * jnp.exp(gc)[..., None]
    k_dec = k * jnp.exp(gc[..., -1:] - gc)[..., None]
    g_last = jnp.exp(gc[..., -1])

    def step(state, inp):
        q_i, k_i, u_i, w_i, a_i, gl_i = inp
        v_new = u_i - jnp.einsum("bhck,bhkv->bhcv", w_i, state)
        o_i = (jnp.einsum("bhck,bhkv->bhcv", q_i, state)
               + jnp.einsum("bhij,bhjv->bhiv", a_i, v_new))
        state = state * gl_i[..., None, None] + jnp.einsum("bhck,bhcv->bhkv", k_i, v_new)
        return state, o_i

    xs = tuple(jnp.moveaxis(t, 2, 0) for t in (q_dec, k_dec, u, w, a_intra, g_last))
    state0 = jnp.zeros((b, nh, dk, dv), q.dtype)
    _, o = lax.scan(step, state0, xs)
    o = jnp.transpose(o, (1, 0, 3, 2, 4))
    return o.reshape(b, s, nh, dv)


def multiscale_pool(u, pool_w, pool_scale):
    bsz, s, _ = u.shape
    uf = u.astype(jnp.float32).reshape(bsz, s, len(POOL_WINDOWS), POOL_GROUP)
    cs = jnp.cumsum(uf, axis=1)
    pos = jnp.arange(s)
    outs = []
    for gi, win in enumerate(POOL_WINDOWS):
        c = cs[:, :, gi]
        prev = jnp.pad(c, ((0, 0), (win, 0), (0, 0)))[:, :s]
        cnt = jnp.minimum(pos + 1, win).astype(jnp.float32)[None, :, None]
        outs.append((c - prev) / cnt - uf[:, :, gi])
    p = jnp.stack(outs, axis=2).astype(u.dtype)
    y = jnp.einsum("bsgc,gcd->bsgd", p, pool_w)
    return y.reshape(bsz, s, D_GROUP) * pool_scale


def setup_inputs(seed: int = 0) -> dict:
    key = jax.random.key(seed)
    ks = jax.random.split(key, 24)
    f32 = jnp.float32

    def nrm(k, shape, scale):
        return jax.random.normal(k, shape, f32) * scale

    def gain(k, shape):
        return 1.0 + 0.02 * jax.random.normal(k, shape, f32)

    dt = jnp.exp(jax.random.uniform(ks[10], (DEPTH, GDN_HEADS), f32,
                                    math.log(1e-3), math.log(1e-1)))
    return {
        "x": jax.random.normal(ks[0], (BATCH, SEQ, D_MODEL), f32),
        "attn_norm_w": gain(ks[1], (DEPTH, D_MODEL)),
        "w_in": nrm(ks[2], (DEPTH, D_MODEL, IN_COLS), D_MODEL ** -0.5),
        "sc_conv_w": nrm(ks[3], (DEPTH, SC_WIDTH, D_GROUP), SC_WIDTH ** -0.5),
        "cf_conv_w": nrm(ks[4], (DEPTH, CF_WIDTH, D_GROUP), CF_WIDTH ** -0.5),
        "cf_conv_b": nrm(ks[5], (DEPTH, D_GROUP), 0.02),
        "cf_ln_w": gain(ks[6], (DEPTH, D_GROUP)),
        "cf_ln_b": nrm(ks[7], (DEPTH, D_GROUP), 0.02),
        "gdn_conv_w": nrm(ks[8], (DEPTH, GDN_CONV, 3 * D_GROUP), GDN_CONV ** -0.5),
        "gdn_a_log": jnp.log(jax.random.uniform(ks[9], (DEPTH, GDN_HEADS), f32, 1.0, 16.0)),
        "gdn_dt_bias": dt + jnp.log(-jnp.expm1(-dt)),
        "gdn_norm_w": gain(ks[11], (DEPTH, GDN_HEAD_DIM)),
        "pool_w": nrm(ks[12], (DEPTH, len(POOL_WINDOWS), POOL_GROUP, POOL_GROUP), POOL_GROUP ** -0.5),
        "pool_scale": gain(ks[13], (DEPTH, D_GROUP)),
        "w_out": nrm(ks[14], (DEPTH, D_MIX, D_MODEL), D_MIX ** -0.5),
        "ffn_norm_w": gain(ks[15], (DEPTH, D_MODEL)),
        "w_gate": nrm(ks[16], (DEPTH, D_MODEL, D_FF), D_MODEL ** -0.5),
        "w_up": nrm(ks[17], (DEPTH, D_MODEL, D_FF), D_MODEL ** -0.5),
        "w_down": nrm(ks[18], (DEPTH, D_FF, D_MODEL), D_FF ** -0.5),
        "final_norm_w": gain(ks[19], (D_MODEL,)),
    }


def reference(x, attn_norm_w, w_in, sc_conv_w, cf_conv_w, cf_conv_b, cf_ln_w, cf_ln_b,
              gdn_conv_w, gdn_a_log, gdn_dt_bias, gdn_norm_w, pool_w, pool_scale,
              w_out, ffn_norm_w, w_gate, w_up, w_down, final_norm_w):
    bsz, s, _ = x.shape
    f32 = jnp.float32
    hs = (bsz, s, GDN_HEADS, GDN_HEAD_DIM)
    for i in range(DEPTH):
        h = rms_norm(x, attn_norm_w[i])
        proj = h @ w_in[i]
        (sc_b, sc_c, sc_h, cf_in, g_q, g_k, g_v, g_z, g_a, g_b, pool_u) = split_cols(proj)

        y_a = sc_b * causal_dwconv(sc_c * sc_h, sc_conv_w[i])

        glu = cf_in[..., :D_GROUP] * jax.nn.sigmoid(cf_in[..., D_GROUP:])
        cf = causal_dwconv(glu, cf_conv_w[i]) + cf_conv_b[i]
        y_b = jax.nn.silu(layer_norm(cf, cf_ln_w[i], cf_ln_b[i]))

        qkv = jax.nn.silu(causal_dwconv(jnp.concatenate([g_q, g_k, g_v], axis=-1), gdn_conv_w[i]))
        q = l2norm(qkv[..., :D_GROUP].reshape(hs))
        k = l2norm(qkv[..., D_GROUP:2 * D_GROUP].reshape(hs))
        v = qkv[..., 2 * D_GROUP:].reshape(hs).astype(f32)
        beta = jax.nn.sigmoid(g_b.astype(f32))
        gdec = -jnp.exp(gdn_a_log[i].astype(f32)) * jax.nn.softplus(
            g_a.astype(f32) + gdn_dt_bias[i].astype(f32))
        o = gated_delta_rule(q, k, v, gdec, beta)
        o = rms_norm(o, gdn_norm_w[i]) * jax.nn.silu(g_z.reshape(hs).astype(f32))
        y_c = o.reshape(bsz, s, D_GROUP).astype(x.dtype)

        y_d = multiscale_pool(pool_u, pool_w[i], pool_scale[i])

        mix = jnp.concatenate([y_a, y_b, y_c, y_d], axis=-1)
        x = x + mix @ w_out[i]

        h = rms_norm(x, ffn_norm_w[i])
        x = x + (jax.nn.silu(h @ w_gate[i]) * (h @ w_up[i])) @ w_down[i]
    return rms_norm(x, final_norm_w)
```

```python
import functools

import jax
import jax.numpy as jnp
from jax import lax
from jax.experimental import pallas as pl
from jax.experimental.pallas import tpu as pltpu

NORM_EPS = 1e-6
GDN_HEAD_DIM = 128
GDN_CHUNK = 64
POOL_WINDOWS = (2, 4, 8, 16)
LANE = 128
SUBLANE = 8
VMEM_LIMIT = 52 * 1024 * 1024

_HI = lax.Precision.HIGHEST


def _cparams(sem):
    return pltpu.CompilerParams(dimension_semantics=sem, vmem_limit_bytes=VMEM_LIMIT)


def _sigmoid(x):
    return 1.0 / (1.0 + jnp.exp(-x))


def _silu(x):
    return x * _sigmoid(x)


def _dot_bf16(a, b, dims=(((1,), (0,)), ((), ()))):
    return lax.dot_general(a.astype(jnp.bfloat16), b.astype(jnp.bfloat16), dims,
                           preferred_element_type=jnp.float32)


def _dot_f32(a, b, dims=(((1,), (0,)), ((), ()))):
    return lax.dot_general(a, b, dims, precision=_HI, preferred_element_type=jnp.float32)


_NT = (((1,), (1,)), ((), ()))
_TN = (((0,), (0,)), ((), ()))


def _rmsnorm_kernel(x_ref, w_ref, o_ref):
    x = x_ref[...]
    ms = jnp.mean(x * x, axis=-1, keepdims=True)
    o_ref[...] = (x * lax.rsqrt(ms + NORM_EPS) * w_ref[...]).astype(o_ref.dtype)


def rmsnorm(x, w, out_dtype, tm=256):
    m, d = x.shape
    return pl.pallas_call(
        _rmsnorm_kernel,
        grid=(m // tm,),
        in_specs=[pl.BlockSpec((tm, d), lambda i: (i, 0)),
                  pl.BlockSpec((1, d), lambda i: (0, 0))],
        out_specs=pl.BlockSpec((tm, d), lambda i: (i, 0)),
        out_shape=jax.ShapeDtypeStruct((m, d), out_dtype),
        compiler_params=_cparams(("parallel",)),
        name="rmsnorm",
    )(x, w.reshape(1, d))


def _mm_kernel(a_ref, b_ref, o_ref):
    o_ref[...] = jnp.dot(a_ref[...], b_ref[...],
                         preferred_element_type=jnp.float32).astype(o_ref.dtype)


def matmul(a, b, out_dtype, tm, tn):
    m, k = a.shape
    n = b.shape[1]
    return pl.pallas_call(
        _mm_kernel,
        grid=(m // tm, n // tn),
        in_specs=[pl.BlockSpec((tm, k), lambda i, j: (i, 0)),
                  pl.BlockSpec((k, tn), lambda i, j: (0, j))],
        out_specs=pl.BlockSpec((tm, tn), lambda i, j: (i, j)),
        out_shape=jax.ShapeDtypeStruct((m, n), out_dtype),
        compiler_params=_cparams(("parallel", "arbitrary")),
        name="matmul",
    )(a, b)


def _mm_res_kernel(a_ref, b_ref, r_ref, o_ref):
    o_ref[...] = r_ref[...] + jnp.dot(a_ref[...], b_ref[...],
                                      preferred_element_type=jnp.float32)


def matmul_residual(a, b, res, tm, tn):
    m, k = a.shape
    n = b.shape[1]
    return pl.pallas_call(
        _mm_res_kernel,
        grid=(m // tm, n // tn),
        in_specs=[pl.BlockSpec((tm, k), lambda i, j: (i, 0)),
                  pl.BlockSpec((k, tn), lambda i, j: (0, j)),
                  pl.BlockSpec((tm, tn), lambda i, j: (i, j))],
        out_specs=pl.BlockSpec((tm, tn), lambda i, j: (i, j)),
        out_shape=jax.ShapeDtypeStruct((m, n), jnp.float32),
        compiler_params=_cparams(("parallel", "arbitrary")),
        name="matmul_residual",
    )(a, b, res)


def _mm4_res_kernel(a0, a1, a2, a3, b_ref, r_ref, o_ref):
    kg = a0.shape[1]
    acc = r_ref[...]
    for g, a in enumerate((a0, a1, a2, a3)):
        acc = acc + jnp.dot(a[...], b_ref[g * kg:(g + 1) * kg, :],
                            preferred_element_type=jnp.float32)
    o_ref[...] = acc


def matmul4_residual(parts, b, res, tm, tn):
    m, kg = parts[0].shape
    k, n = b.shape
    a_spec = pl.BlockSpec((tm, kg), lambda i, j: (i, 0))
    return pl.pallas_call(
        _mm4_res_kernel,
        grid=(m // tm, n // tn),
        in_specs=[a_spec, a_spec, a_spec, a_spec,
                  pl.BlockSpec((k, tn), lambda i, j: (0, j)),
                  pl.BlockSpec((tm, tn), lambda i, j: (i, j))],
        out_specs=pl.BlockSpec((tm, tn), lambda i, j: (i, j)),
        out_shape=jax.ShapeDtypeStruct((m, n), jnp.float32),
        compiler_params=_cparams(("parallel", "arbitrary")),
        name="out_proj",
    )(*parts, b, res)


def _mm_glu_kernel(a_ref, bg_ref, bu_ref, o_ref):
    a = a_ref[...]
    g = jnp.dot(a, bg_ref[...], preferred_element_type=jnp.float32)
    u = jnp.dot(a, bu_ref[...], preferred_element_type=jnp.float32)
    o_ref[...] = (_silu(g) * u).astype(o_ref.dtype)


def matmul_swiglu(a, bg, bu, tm, tn):
    m, k = a.shape
    n = bg.shape[1]
    b_spec = pl.BlockSpec((k, tn), lambda i, j: (0, j))
    return pl.pallas_call(
        _mm_glu_kernel,
        grid=(m // tm, n // tn),
        in_specs=[pl.BlockSpec((tm, k), lambda i, j: (i, 0)), b_spec, b_spec],
        out_specs=pl.BlockSpec((tm, tn), lambda i, j: (i, j)),
        out_shape=jax.ShapeDtypeStruct((m, n), jnp.bfloat16),
        compiler_params=_cparams(("parallel", "arbitrary")),
        name="ffn_gate_up",
    )(a, bg, bu)


def _halo_spec(ts, hb, c, col):
    r = ts // hb
    return pl.BlockSpec((hb, c), lambda i: (jnp.maximum(i * r - 1, 0), col))


def _main_spec(ts, c, col):
    return pl.BlockSpec((ts, c), lambda i: (i, col))


def _fill_history(scr, first, halo, cur, hb):
    scr[0:hb, :] = jnp.where(first, 0.0, halo)
    scr[hb:, :] = cur


def _mix_sconv_kernel(b_ref, c_ref, h_ref, ch_ref, hh_ref, w_ref, o_ref, scr, *, tiles_per_seq):
    ts = o_ref.shape[0]
    first = (pl.program_id(0) % tiles_per_seq) == 0
    m = c_ref[...] * h_ref[...]
    _fill_history(scr, first, ch_ref[...] * hh_ref[...], m, SUBLANE)
    w = w_ref[...]
    kw = w.shape[0]
    y = w[kw - 1:kw, :] * m
    for d in range(1, kw):
        y = y + w[kw - 1 - d:kw - d, :] * scr[SUBLANE - d:SUBLANE - d + ts, :]
    o_ref[...] = (b_ref[...] * y).astype(o_ref.dtype)


def mix_sconv(proj, w, seq, cols, ts=256):
    m = proj.shape[0]
    c = w.shape[1]
    kern = functools.partial(_mix_sconv_kernel, tiles_per_seq=seq // ts)
    return pl.pallas_call(
        kern,
        grid=(m // ts,),
        in_specs=[_main_spec(ts, c, cols[0]), _main_spec(ts, c, cols[1]), _main_spec(ts, c, cols[2]),
                  _halo_spec(ts, SUBLANE, c, cols[1]), _halo_spec(ts, SUBLANE, c, cols[2]),
                  pl.BlockSpec(w.shape, lambda i: (0, 0))],
        out_specs=pl.BlockSpec((ts, c), lambda i: (i, 0)),
        out_shape=jax.ShapeDtypeStruct((m, c), jnp.bfloat16),
        scratch_shapes=[pltpu.VMEM((SUBLANE + ts, c), jnp.float32)],
        compiler_params=_cparams(("parallel",)),
        name="mix_sconv",
    )(proj, proj, proj, proj, proj, w)


CF_HALO = 32


def _mix_conformer_kernel(v_ref, g_ref, vh_ref, gh_ref, w_ref, cb_ref, lw_ref, lb_ref, o_ref,
                          scr, cf_scr, *, tiles_per_seq):
    ts, c = o_ref.shape
    first = (pl.program_id(0) % tiles_per_seq) == 0
    glu = v_ref[...] * _sigmoid(g_ref[...])
    _fill_history(scr, first, vh_ref[...] * _sigmoid(gh_ref[...]), glu, CF_HALO)
    kw = w_ref.shape[0]
    s1 = jnp.zeros((ts, LANE), jnp.float32)
    for c0 in range(0, c, LANE):
        acc = jnp.zeros((ts, LANE), jnp.float32) + cb_ref[:, c0:c0 + LANE]
        for d in range(kw):
            acc = acc + w_ref[kw - 1 - d:kw - d, c0:c0 + LANE] * scr[CF_HALO - d:CF_HALO - d + ts, c0:c0 + LANE]
        cf_scr[:, c0:c0 + LANE] = acc
        s1 = s1 + acc
    mu = jnp.sum(s1, axis=-1, keepdims=True) * (1.0 / c)
    s2 = jnp.zeros((ts, LANE), jnp.float32)
    for c0 in range(0, c, LANE):
        xc = cf_scr[:, c0:c0 + LANE] - mu
        s2 = s2 + xc * xc
    rstd = lax.rsqrt(jnp.sum(s2, axis=-1, keepdims=True) * (1.0 / c) + NORM_EPS)
    for c0 in range(0, c, LANE):
        y = (cf_scr[:, c0:c0 + LANE] - mu) * rstd * lw_ref[:, c0:c0 + LANE] + lb_ref[:, c0:c0 + LANE]
        o_ref[:, c0:c0 + LANE] = _silu(y).astype(o_ref.dtype)


def mix_conformer(proj, w, cb, lw, lb, seq, cols, ts=128):
    m = proj.shape[0]
    c = w.shape[1]
    kern = functools.partial(_mix_conformer_kernel, tiles_per_seq=seq // ts)
    vec = pl.BlockSpec((1, c), lambda i: (0, 0))
    return pl.pallas_call(
        kern,
        grid=(m // ts,),
        in_specs=[_main_spec(ts, c, cols[0]), _main_spec(ts, c, cols[1]),
                  _halo_spec(ts, CF_HALO, c, cols[0]), _halo_spec(ts, CF_HALO, c, cols[1]),
                  pl.BlockSpec(w.shape, lambda i: (0, 0)), vec, vec, vec],
        out_specs=pl.BlockSpec((ts, c), lambda i: (i, 0)),
        out_shape=jax.ShapeDtypeStruct((m, c), jnp.bfloat16),
        scratch_shapes=[pltpu.VMEM((CF_HALO + ts, c), jnp.float32),
                        pltpu.VMEM((ts, c), jnp.float32)],
        compiler_params=_cparams(("parallel",)),
        name="mix_conformer",
    )(proj, proj, proj, proj, w, cb.reshape(1, c), lw.reshape(1, c), lb.reshape(1, c))


POOL_HALO = 16


def _mix_pool_kernel(u_ref, uh_ref, pw_ref, ps_ref, o_ref, scr, *, tiles_per_seq):
    ts, c = o_ref.shape
    ng = pw_ref.shape[0]
    cg = c // ng
    tile = pl.program_id(0) % tiles_per_seq
    first = tile == 0
    _fill_history(scr, first, uh_ref[...], u_ref[...], POOL_HALO)
    pos = tile * ts + lax.broadcasted_iota(jnp.int32, (ts, 1), 0)
    for gi, win in enumerate(POOL_WINDOWS):
        c0 = gi * cg
        cur = scr[POOL_HALO:POOL_HALO + ts, c0:c0 + cg]
        acc = cur
        for d in range(1, win):
            acc = acc + scr[POOL_HALO - d:POOL_HALO - d + ts, c0:c0 + cg]
        cnt = jnp.minimum(pos + 1, win).astype(jnp.float32)
        p = acc / cnt - cur
        y = _dot_bf16(p, pw_ref[gi])
        o_ref[:, c0:c0 + cg] = (y * ps_ref[:, c0:c0 + cg]).astype(o_ref.dtype)


def mix_pool(proj, pool_w, pool_scale, seq, col, ts=256):
    m = proj.shape[0]
    ng, cg, _ = pool_w.shape
    c = ng * cg
    kern = functools.partial(_mix_pool_kernel, tiles_per_seq=seq // ts)
    return pl.pallas_call(
        kern,
        grid=(m // ts,),
        in_specs=[_main_spec(ts, c, col), _halo_spec(ts, POOL_HALO, c, col),
                  pl.BlockSpec(pool_w.shape, lambda i: (0, 0, 0)),
                  pl.BlockSpec((1, c), lambda i: (0, 0))],
        out_specs=pl.BlockSpec((ts, c), lambda i: (i, 0)),
        out_shape=jax.ShapeDtypeStruct((m, c), jnp.bfloat16),
        scratch_shapes=[pltpu.VMEM((POOL_HALO + ts, c), jnp.float32)],
        compiler_params=_cparams(("parallel",)),
        name="mix_pool",
    )(proj, proj, pool_w.astype(jnp.bfloat16), pool_scale.reshape(1, c))


def _tri_inverse(mat, eye):
    n = mat.shape[0]
    p = -mat
    x = eye + p
    steps = max(1, (n - 1).bit_length())
    for _ in range(steps - 1):
        p = _dot_f32(p, p)
        x = x + _dot_f32(x, p)
    return x


def _gdn_kernel(q_ref, k_ref, v_ref, qh_ref, kh_ref, vh_ref, z_ref, ab_ref,
                wq_ref, wk_ref, wv_ref, alog_ref, dtb_ref, nw_ref, o_ref,
                st_ref, hist, qs, ks, vs, *, n_heads):
    ts = o_ref.shape[0]
    dh = GDN_HEAD_DIM
    cc = GDN_CHUNK
    first = pl.program_id(1) == 0

    @pl.when(first)
    def _():
        st_ref[...] = jnp.zeros_like(st_ref)

    def conv_silu(x_ref, xh_ref, w_ref, dst):
        _fill_history(hist, first, xh_ref[...], x_ref[...], SUBLANE)
        kw = w_ref.shape[0]
        acc = w_ref[kw - 1:kw, :] * x_ref[...]
        for d in range(1, kw):
            acc = acc + w_ref[kw - 1 - d:kw - d, :] * hist[SUBLANE - d:SUBLANE - d + ts, :]
        dst[...] = _silu(acc)

    conv_silu(q_ref, qh_ref, wq_ref, qs)
    conv_silu(k_ref, kh_ref, wk_ref, ks)
    conv_silu(v_ref, vh_ref, wv_ref, vs)

    ab = ab_ref[...]
    xg = ab + dtb_ref[...]
    softplus = jnp.maximum(xg, 0.0) + jnp.log1p(jnp.exp(-jnp.abs(xg)))
    g_all = -jnp.exp(alog_ref[...]) * softplus
    beta_all = _sigmoid(ab)

    row = lax.broadcasted_iota(jnp.int32, (cc, cc), 0)
    col = lax.broadcasted_iota(jnp.int32, (cc, cc), 1)
    lower = (row >= col).astype(jnp.float32)
    strict = (row > col).astype(jnp.float32)
    eye = (row == col).astype(jnp.float32)
    scale = dh ** -0.5

    for h in range(n_heads):
        g_col = jnp.broadcast_to(g_all[:, h:h + 1], (ts, dh))
        b_col = jnp.broadcast_to(beta_all[:, n_heads + h:n_heads + h + 1], (ts, dh))
        lanes = slice(h * dh, (h + 1) * dh)
        for c in range(ts // cc):
            rows = slice(c * cc, (c + 1) * cc)
            q = qs[rows, lanes]
            k = ks[rows, lanes]
            v = vs[rows, lanes]
            g = g_col[rows]
            beta = b_col[rows]
            q = q * (lax.rsqrt(jnp.sum(q * q, axis=-1, keepdims=True) + NORM_EPS) * scale)
            k = k * lax.rsqrt(jnp.sum(k * k, axis=-1, keepdims=True) + NORM_EPS)
            gc = _dot_f32(lower, g)
            pdiff = _dot_f32(lower, g[:, :cc] * strict)
            decay = jnp.where(row >= col, jnp.exp(pdiff), 0.0)
            egc = jnp.exp(gc)
            gc_last = gc[cc - 1:cc, :]
            kb = k * beta
            vb = v * beta
            mkk = jnp.where(row > col, _dot_f32(kb, k, _NT) * decay, 0.0)
            t_inv = _tri_inverse(mkk, eye)
            u = _dot_f32(t_inv, vb)
            w = _dot_f32(t_inv, kb * egc)
            a_intra = _dot_f32(q, k, _NT) * decay
            q_dec = q * egc
            k_dec = k * jnp.exp(gc_last - gc)
            state = st_ref[h]
            v_new = u - _dot_f32(w, state)
            o = _dot_f32(q_dec, state) + _dot_f32(a_intra, v_new)
            st_ref[h] = state * jnp.exp(gc_last) + _dot_f32(k_dec, v_new, _TN)
            o = o * lax.rsqrt(jnp.mean(o * o, axis=-1, keepdims=True) + NORM_EPS) * nw_ref[...]
            o_ref[rows, lanes] = (o * _silu(z_ref[rows, lanes])).astype(o_ref.dtype)


def mix_gdn(proj, ab, conv_w, a_log, dt_bias, norm_w, batch, seq, cols, ts=128):
    m = proj.shape[0]
    c = conv_w.shape[1] // 3
    n_heads = c // GDN_HEAD_DIM
    nt = seq // ts
    r = ts // SUBLANE

    def main(colblk):
        return pl.BlockSpec((ts, c), lambda b, s: (b * nt + s, colblk))

    def halo(colblk):
        return pl.BlockSpec((SUBLANE, c), lambda b, s: (jnp.maximum((b * nt + s) * r - 1, 0), colblk))

    def wspec(j):
        return pl.BlockSpec((conv_w.shape[0], c), lambda b, s: (0, j))

    vec = pl.BlockSpec((1, LANE), lambda b, s: (0, 0))
    pad = LANE - n_heads
    alog = jnp.pad(a_log, (0, pad)).reshape(1, LANE)
    dtb = jnp.pad(dt_bias, (0, pad)).reshape(1, LANE)
    kern = functools.partial(_gdn_kernel, n_heads=n_heads)
    return pl.pallas_call(
        kern,
        grid=(batch, nt),
        in_specs=[main(cols[0]), main(cols[1]), main(cols[2]),
                  halo(cols[0]), halo(cols[1]), halo(cols[2]),
                  main(cols[3]),
                  pl.BlockSpec((ts, LANE), lambda b, s: (b * nt + s, 0)),
                  wspec(0), wspec(1), wspec(2), vec, vec, vec],
        out_specs=pl.BlockSpec((ts, c), lambda b, s: (b * nt + s, 0)),
        out_shape=jax.ShapeDtypeStruct((m, c), jnp.bfloat16),
        scratch_shapes=[pltpu.VMEM((n_heads, GDN_HEAD_DIM, GDN_HEAD_DIM), jnp.float32),
                        pltpu.VMEM((SUBLANE + ts, c), jnp.float32),
                        pltpu.VMEM((ts, c), jnp.float32),
                        pltpu.VMEM((ts, c), jnp.float32),
                        pltpu.VMEM((ts, c), jnp.float32)],
        compiler_params=_cparams(("arbitrary", "arbitrary")),
        name="mix_gdn",
    )(proj, proj, proj, proj, proj, proj, proj, ab,
      conv_w, conv_w, conv_w, alog, dtb, norm_w.reshape(1, LANE))


FF_TILE = 1024


def _layer(x2, batch, seq, p):
    d = x2.shape[1]
    cg = p["sc_conv_w"].shape[1]
    n_heads = cg // GDN_HEAD_DIM
    bf = jnp.bfloat16

    w_in = p["w_in"]
    ab_lo = 9 * cg
    ab_hi = ab_lo + 2 * n_heads
    w_main = jnp.concatenate([w_in[:, :ab_lo], w_in[:, ab_hi:]], axis=1).astype(bf)
    w_ab = jnp.pad(w_in[:, ab_lo:ab_hi], ((0, 0), (0, LANE - 2 * n_heads))).astype(bf)

    h = rmsnorm(x2, p["attn_norm_w"], bf)
    proj = matmul(h, w_main, jnp.float32, tm=1024, tn=512)
    ab = matmul(h, w_ab, jnp.float32, tm=1024, tn=LANE)

    y_a = mix_sconv(proj, p["sc_conv_w"], seq, cols=(0, 1, 2))
    y_b = mix_conformer(proj, p["cf_conv_w"], p["cf_conv_b"], p["cf_ln_w"], p["cf_ln_b"], seq, cols=(3, 4))
    y_c = mix_gdn(proj, ab, p["gdn_conv_w"], p["gdn_a_log"], p["gdn_dt_bias"], p["gdn_norm_w"],
                  batch, seq, cols=(5, 6, 7, 8))
    y_d = mix_pool(proj, p["pool_w"], p["pool_scale"], seq, col=9)

    x2 = matmul4_residual((y_a, y_b, y_c, y_d), p["w_out"].astype(bf), x2, tm=1024, tn=512)

    d_ff = p["w_gate"].shape[1]
    ff_pad = -d_ff % FF_TILE
    wg = jnp.pad(p["w_gate"], ((0, 0), (0, ff_pad))).astype(bf)
    wu = jnp.pad(p["w_up"], ((0, 0), (0, ff_pad))).astype(bf)
    wd = jnp.pad(p["w_down"], ((0, ff_pad), (0, 0))).astype(bf)
    h = rmsnorm(x2, p["ffn_norm_w"], bf)
    act = matmul_swiglu(h, wg, wu, tm=1024, tn=512)
    x2 = matmul_residual(act, wd, x2, tm=512, tn=256)
    return x2


def kernel(x, attn_norm_w, w_in, sc_conv_w, cf_conv_w, cf_conv_b, cf_ln_w, cf_ln_b, gdn_conv_w,
           gdn_a_log, gdn_dt_bias, gdn_norm_w, pool_w, pool_scale, w_out, ffn_norm_w, w_gate, w_up,
           w_down, final_norm_w):
    batch, seq, d = x.shape
    layers = dict(attn_norm_w=attn_norm_w, w_in=w_in, sc_conv_w=sc_conv_w, cf_conv_w=cf_conv_w,
                  cf_conv_b=cf_conv_b, cf_ln_w=cf_ln_w, cf_ln_b=cf_ln_b, gdn_conv_w=gdn_conv_w,
                  gdn_a_log=gdn_a_log, gdn_dt_bias=gdn_dt_bias, gdn_norm_w=gdn_norm_w, pool_w=pool_w,
                  pool_scale=pool_scale, w_out=w_out, ffn_norm_w=ffn_norm_w, w_gate=w_gate, w_up=w_up,
                  w_down=w_down)
    x2 = x.reshape(batch * seq, d)
    for i in range(attn_norm_w.shape[0]):
        x2 = _layer(x2, batch, seq, {name: val[i] for name, val in layers.items()})
    out = rmsnorm(x2, final_norm_w, x.dtype)
    return out.reshape(batch, seq, d)
```

```python
import functools

import jax
import jax.numpy as jnp
from jax import lax
from jax.experimental import pallas as pl
from jax.experimental.pallas import tpu as pltpu

NORM_EPS = 1e-6
GDN_HEAD_DIM = 128
GDN_CHUNK = 64
POOL_WINDOWS = (2, 4, 8, 16)
LANE = 128
SUBLANE = 8
VMEM_LIMIT = 56 * 1024 * 1024

BF16 = jnp.bfloat16
F32 = jnp.float32


def _cparams(sem):
    return pltpu.CompilerParams(dimension_semantics=sem, vmem_limit_bytes=VMEM_LIMIT)


def _sigmoid(x):
    return 1.0 / (1.0 + jnp.exp(-x))


def _silu(x):
    return x * _sigmoid(x)


_NN = (((1,), (0,)), ((), ()))
_NT = (((1,), (1,)), ((), ()))
_TN = (((0,), (0,)), ((), ()))


def _dot_bf16(a, b, dims=_NN):
    return lax.dot_general(a.astype(BF16), b.astype(BF16), dims, preferred_element_type=F32)


def _cast_kernel(x_ref, o_ref):
    o_ref[...] = x_ref[:, :o_ref.shape[1]].astype(o_ref.dtype)


def cast_weight(w_stacked, layer, n_out, tk=128):
    _, k, n = w_stacked.shape
    return pl.pallas_call(
        _cast_kernel,
        grid=(k // tk,),
        in_specs=[pl.BlockSpec((None, tk, n), lambda r: (layer, r, 0))],
        out_specs=pl.BlockSpec((tk, n_out), lambda r: (r, 0)),
        out_shape=jax.ShapeDtypeStruct((k, n_out), BF16),
        compiler_params=_cparams(("parallel",)),
        name="cast_weight",
    )(w_stacked)


def _rmsnorm_kernel(x_ref, w_ref, o_ref):
    x = x_ref[...]
    ms = jnp.mean(x * x, axis=-1, keepdims=True)
    o_ref[...] = (x * lax.rsqrt(ms + NORM_EPS) * w_ref[...]).astype(o_ref.dtype)


def rmsnorm(x, w, out_dtype, tm=256):
    m, d = x.shape
    return pl.pallas_call(
        _rmsnorm_kernel,
        grid=(m // tm,),
        in_specs=[pl.BlockSpec((tm, d), lambda i: (i, 0)),
                  pl.BlockSpec((1, d), lambda i: (0, 0))],
        out_specs=pl.BlockSpec((tm, d), lambda i: (i, 0)),
        out_shape=jax.ShapeDtypeStruct((m, d), out_dtype),
        compiler_params=_cparams(("parallel",)),
        name="rmsnorm",
    )(x, w.reshape(1, d))


def _panel_spec(tm, k):
    return pl.BlockSpec((tm, k), lambda i, j: (i, 0), pipeline_mode=pl.Buffered(1))


def _mm_kernel(a_ref, b_ref, o_ref):
    o_ref[...] = jnp.dot(a_ref[...], b_ref[...], preferred_element_type=F32).astype(o_ref.dtype)


def matmul(a, b, out_dtype, tm, tn):
    m, k = a.shape
    n = b.shape[1]
    return pl.pallas_call(
        _mm_kernel,
        grid=(m // tm, n // tn),
        in_specs=[_panel_spec(tm, k),
                  pl.BlockSpec((k, tn), lambda i, j: (0, j))],
        out_specs=pl.BlockSpec((tm, tn), lambda i, j: (i, j)),
        out_shape=jax.ShapeDtypeStruct((m, n), out_dtype),
        compiler_params=_cparams(("parallel", "arbitrary")),
        name="matmul",
    )(a, b)


def _mm_res_kernel(a_ref, b_ref, r_ref, o_ref):
    o_ref[...] = r_ref[...] + jnp.dot(a_ref[...], b_ref[...], preferred_element_type=F32)


def matmul_residual(a, b, res, tm, tn):
    m, k = a.shape
    n = b.shape[1]
    return pl.pallas_call(
        _mm_res_kernel,
        grid=(m // tm, n // tn),
        in_specs=[pl.BlockSpec((tm, k), lambda i, j: (i, 0)),
                  pl.BlockSpec((k, tn), lambda i, j: (0, j)),
                  pl.BlockSpec((tm, tn), lambda i, j: (i, j))],
        out_specs=pl.BlockSpec((tm, tn), lambda i, j: (i, j)),
        out_shape=jax.ShapeDtypeStruct((m, n), F32),
        compiler_params=_cparams(("parallel", "arbitrary")),
        name="matmul_residual",
    )(a, b, res)


def _mm4_res_kernel(a0, a1, a2, a3, b_ref, r_ref, o_ref):
    kg = a0.shape[1]
    acc = r_ref[...]
    for g, a in enumerate((a0, a1, a2, a3)):
        acc = acc + jnp.dot(a[...], b_ref[g * kg:(g + 1) * kg, :], preferred_element_type=F32)
    o_ref[...] = acc


def matmul4_residual(parts, b, res, tm, tn):
    m, kg = parts[0].shape
    k, n = b.shape
    a_spec = pl.BlockSpec((tm, kg), lambda i, j: (i, 0))
    return pl.pallas_call(
        _mm4_res_kernel,
        grid=(m // tm, n // tn),
        in_specs=[a_spec, a_spec, a_spec, a_spec,
                  pl.BlockSpec((k, tn), lambda i, j: (0, j)),
                  pl.BlockSpec((tm, tn), lambda i, j: (i, j))],
        out_specs=pl.BlockSpec((tm, tn), lambda i, j: (i, j)),
        out_shape=jax.ShapeDtypeStruct((m, n), F32),
        compiler_params=_cparams(("parallel", "arbitrary")),
        name="out_proj",
    )(*parts, b, res)


def _mm_glu_kernel(a_ref, bg_ref, bu_ref, o_ref):
    a = a_ref[...]
    g = jnp.dot(a, bg_ref[...], preferred_element_type=F32)
    u = jnp.dot(a, bu_ref[...], preferred_element_type=F32)
    o_ref[...] = (_silu(g) * u).astype(o_ref.dtype)


def matmul_swiglu(a, bg, bu, tm, tn):
    m, k = a.shape
    n = bg.shape[1]
    b_spec = pl.BlockSpec((k, tn), lambda i, j: (0, j))
    return pl.pallas_call(
        _mm_glu_kernel,
        grid=(m // tm, n // tn),
        in_specs=[_panel_spec(tm, k), b_spec, b_spec],
        out_specs=pl.BlockSpec((tm, tn), lambda i, j: (i, j)),
        out_shape=jax.ShapeDtypeStruct((m, n), BF16),
        compiler_params=_cparams(("parallel", "arbitrary")),
        name="ffn_gate_up",
    )(a, bg, bu)


def _halo_spec(ts, hb, c, col):
    r = ts // hb
    return pl.BlockSpec((hb, c), lambda i: (jnp.maximum(i * r - 1, 0), col))


def _main_spec(ts, c, col):
    return pl.BlockSpec((ts, c), lambda i: (i, col))


def _fill_history(scr, first, halo, cur, hb):
    scr[0:hb, :] = jnp.where(first, 0.0, halo)
    scr[hb:, :] = cur


def _mix_sconv_kernel(b_ref, c_ref, h_ref, ch_ref, hh_ref, w_ref, o_ref, scr, *, tiles_per_seq):
    ts = o_ref.shape[0]
    first = (pl.program_id(0) % tiles_per_seq) == 0
    m = c_ref[...] * h_ref[...]
    _fill_history(scr, first, ch_ref[...] * hh_ref[...], m, SUBLANE)
    w = w_ref[...]
    kw = w.shape[0]
    y = w[kw - 1:kw, :] * m
    for d in range(1, kw):
        y = y + w[kw - 1 - d:kw - d, :] * scr[SUBLANE - d:SUBLANE - d + ts, :]
    o_ref[...] = (b_ref[...] * y).astype(o_ref.dtype)


def mix_sconv(sb, sc, sh, w, seq, ts=256):
    m = sb[0].shape[0]
    c = w.shape[1]
    kern = functools.partial(_mix_sconv_kernel, tiles_per_seq=seq // ts)
    return pl.pallas_call(
        kern,
        grid=(m // ts,),
        in_specs=[_main_spec(ts, c, sb[1]), _main_spec(ts, c, sc[1]), _main_spec(ts, c, sh[1]),
                  _halo_spec(ts, SUBLANE, c, sc[1]), _halo_spec(ts, SUBLANE, c, sh[1]),
                  pl.BlockSpec(w.shape, lambda i: (0, 0))],
        out_specs=pl.BlockSpec((ts, c), lambda i: (i, 0)),
        out_shape=jax.ShapeDtypeStruct((m, c), BF16),
        scratch_shapes=[pltpu.VMEM((SUBLANE + ts, c), F32)],
        compiler_params=_cparams(("parallel",)),
        name="mix_sconv",
    )(sb[0], sc[0], sh[0], sc[0], sh[0], w)


CF_HALO = 32


def _mix_conformer_kernel(v_ref, g_ref, vh_ref, gh_ref, w_ref, cb_ref, lw_ref, lb_ref, o_ref,
                          scr, cf_scr, *, tiles_per_seq):
    ts, c = o_ref.shape
    first = (pl.program_id(0) % tiles_per_seq) == 0
    glu = v_ref[...] * _sigmoid(g_ref[...])
    _fill_history(scr, first, vh_ref[...] * _sigmoid(gh_ref[...]), glu, CF_HALO)
    kw = w_ref.shape[0]
    s1 = jnp.zeros((ts, LANE), F32)
    for c0 in range(0, c, LANE):
        acc = jnp.zeros((ts, LANE), F32) + cb_ref[:, c0:c0 + LANE]
        for d in range(kw):
            acc = acc + w_ref[kw - 1 - d:kw - d, c0:c0 + LANE] * scr[CF_HALO - d:CF_HALO - d + ts, c0:c0 + LANE]
        cf_scr[:, c0:c0 + LANE] = acc
        s1 = s1 + acc
    mu = jnp.sum(s1, axis=-1, keepdims=True) * (1.0 / c)
    s2 = jnp.zeros((ts, LANE), F32)
    for c0 in range(0, c, LANE):
        xc = cf_scr[:, c0:c0 + LANE] - mu
        s2 = s2 + xc * xc
    rstd = lax.rsqrt(jnp.sum(s2, axis=-1, keepdims=True) * (1.0 / c) + NORM_EPS)
    for c0 in range(0, c, LANE):
        y = (cf_scr[:, c0:c0 + LANE] - mu) * rstd * lw_ref[:, c0:c0 + LANE] + lb_ref[:, c0:c0 + LANE]
        o_ref[:, c0:c0 + LANE] = _silu(y).astype(o_ref.dtype)


def mix_conformer(sv, sg, w, cb, lw, lb, seq, ts=128):
    m = sv[0].shape[0]
    c = w.shape[1]
    kern = functools.partial(_mix_conformer_kernel, tiles_per_seq=seq // ts)
    vec = pl.BlockSpec((1, c), lambda i: (0, 0))
    return pl.pallas_call(
        kern,
        grid=(m // ts,),
        in_specs=[_main_spec(ts, c, sv[1]), _main_spec(ts, c, sg[1]),
                  _halo_spec(ts, CF_HALO, c, sv[1]), _halo_spec(ts, CF_HALO, c, sg[1]),
                  pl.BlockSpec(w.shape, lambda i: (0, 0)), vec, vec, vec],
        out_specs=pl.BlockSpec((ts, c), lambda i: (i, 0)),
        out_shape=jax.ShapeDtypeStruct((m, c), BF16),
        scratch_shapes=[pltpu.VMEM((CF_HALO + ts, c), F32),
                        pltpu.VMEM((ts, c), F32)],
        compiler_params=_cparams(("parallel",)),
        name="mix_conformer",
    )(sv[0], sg[0], sv[0], sg[0], w, cb.reshape(1, c), lw.reshape(1, c), lb.reshape(1, c))


POOL_HALO = 16


def _mix_pool_kernel(u_ref, uh_ref, pw_ref, ps_ref, o_ref, scr, *, tiles_per_seq):
    ts, c = o_ref.shape
    ng = pw_ref.shape[0]
    cg = c // ng
    tile = pl.program_id(0) % tiles_per_seq
    first = tile == 0
    _fill_history(scr, first, uh_ref[...], u_ref[...], POOL_HALO)
    pos = tile * ts + lax.broadcasted_iota(jnp.int32, (ts, 1), 0)
    for gi, win in enumerate(POOL_WINDOWS):
        c0 = gi * cg
        cur = scr[POOL_HALO:POOL_HALO + ts, c0:c0 + cg]
        acc = cur
        for d in range(1, win):
            acc = acc + scr[POOL_HALO - d:POOL_HALO - d + ts, c0:c0 + cg]
        cnt = jnp.minimum(pos + 1, win).astype(F32)
        p = acc / cnt - cur
        y = _dot_bf16(p, pw_ref[gi])
        o_ref[:, c0:c0 + cg] = (y * ps_ref[:, c0:c0 + cg]).astype(o_ref.dtype)


def mix_pool(su, pool_w, pool_scale, seq, ts=256):
    m = su[0].shape[0]
    ng, cg, _ = pool_w.shape
    c = ng * cg
    kern = functools.partial(_mix_pool_kernel, tiles_per_seq=seq // ts)
    return pl.pallas_call(
        kern,
        grid=(m // ts,),
        in_specs=[_main_spec(ts, c, su[1]), _halo_spec(ts, POOL_HALO, c, su[1]),
                  pl.BlockSpec(pool_w.shape, lambda i: (0, 0, 0)),
                  pl.BlockSpec((1, c), lambda i: (0, 0))],
        out_specs=pl.BlockSpec((ts, c), lambda i: (i, 0)),
        out_shape=jax.ShapeDtypeStruct((m, c), BF16),
        scratch_shapes=[pltpu.VMEM((POOL_HALO + ts, c), F32)],
        compiler_params=_cparams(("parallel",)),
        name="mix_pool",
    )(su[0], su[0], pool_w.astype(BF16), pool_scale.reshape(1, c))


def _split3(x):
    x1 = x.astype(BF16)
    r1 = x - x1.astype(F32)
    x2 = r1.astype(BF16)
    x3 = (r1 - x2.astype(F32)).astype(BF16)
    return x1, x2, x3


P1_GROUP = 16


def _gdn_kernel(q_ref, k_ref, v_ref, qh_ref, kh_ref, vh_ref, z_ref, ab_ref,
                wq_ref, wk_ref, wv_ref, alog_ref, dtb_ref, nw_ref, o_ref,
                st_ref, hist, qs, ks, vs, gc_s, gcr_s, kn_s, kbq_s, vk_s, wqd_s, kd_s, u_s, a_s,
                *, n_heads):
    ts = o_ref.shape[0]
    dh = GDN_HEAD_DIM
    cc = GDN_CHUNK
    n_chunks = ts // cc
    first = pl.program_id(1) == 0

    @pl.when(first)
    def _():
        st_ref[...] = jnp.zeros_like(st_ref)

    def conv_silu(x_ref, xh_ref, w_ref, dst):
        _fill_history(hist, first, xh_ref[...], x_ref[...], SUBLANE)
        kw = w_ref.shape[0]
        acc = w_ref[kw - 1:kw, :] * x_ref[...]
        for d in range(1, kw):
            acc = acc + w_ref[kw - 1 - d:kw - d, :] * hist[SUBLANE - d:SUBLANE - d + ts, :]
        dst[...] = _silu(acc)

    conv_silu(q_ref, qh_ref, wq_ref, qs)
    conv_silu(k_ref, kh_ref, wk_ref, ks)
    conv_silu(v_ref, vh_ref, wv_ref, vs)

    ab = ab_ref[...]
    xg = ab + dtb_ref[...]
    softplus = jnp.maximum(xg, 0.0) + jnp.log1p(jnp.exp(-jnp.abs(xg)))
    g_all = -jnp.exp(alog_ref[...]) * softplus
    beta_all = _sigmoid(ab)

    rt = lax.broadcasted_iota(jnp.int32, (ts, ts), 0)
    ct = lax.broadcasted_iota(jnp.int32, (ts, ts), 1)
    tri_blk = jnp.where((rt >= ct) & ((rt // cc) == (ct // cc)), 1.0, 0.0).astype(BF16)
    g1, g2, g3 = _split3(g_all)
    gc_all = (jnp.dot(tri_blk, g1, preferred_element_type=F32)
              + jnp.dot(tri_blk, g2, preferred_element_type=F32)
              + jnp.dot(tri_blk, g3, preferred_element_type=F32))
    gc_s[...] = gc_all
    gcr_s[...] = gc_all.T

    scale = dh ** -0.5

    for h in range(n_heads):
        lanes = slice(h * dh, (h + 1) * dh)
        gc_col = jnp.broadcast_to(gc_all[:, h:h + 1], (ts, dh))
        b_col = jnp.broadcast_to(beta_all[:, n_heads + h:n_heads + h + 1], (ts, dh))
        q_all = qs[:, lanes]
        k_all = ks[:, lanes]
        q_all = q_all * (lax.rsqrt(jnp.sum(q_all * q_all, axis=-1, keepdims=True) + NORM_EPS) * scale)
        k_all = k_all * lax.rsqrt(jnp.sum(k_all * k_all, axis=-1, keepdims=True) + NORM_EPS)
        egc_all = jnp.exp(gc_col)
        kb_all = k_all * b_col
        kn_s[:, lanes] = k_all.astype(BF16)
        vk_s[:, 2 * h * dh:(2 * h + 1) * dh] = (vs[:, lanes] * b_col).astype(BF16)
        vk_s[:, (2 * h + 1) * dh:(2 * h + 2) * dh] = (kb_all * egc_all).astype(BF16)
        qd_all = q_all * egc_all
        for c in range(n_chunks):
            rows = slice(c * cc, (c + 1) * cc)
            gc_last = gc_col[(c + 1) * cc - 1:(c + 1) * cc, :]
            kbq_s[2 * c * cc:(2 * c + 1) * cc, lanes] = kb_all[rows].astype(BF16)
            kbq_s[(2 * c + 1) * cc:(2 * c + 2) * cc, lanes] = q_all[rows].astype(BF16)
            wqd_s[(2 * c + 1) * cc:(2 * c + 2) * cc, lanes] = qd_all[rows].astype(BF16)
            kd_s[rows, lanes] = (k_all[rows] * jnp.exp(gc_last - gc_col[rows])).astype(BF16)

    row = lax.broadcasted_iota(jnp.int32, (cc, cc), 0)
    col = lax.broadcasted_iota(jnp.int32, (cc, cc), 1)
    eye = (row == col).astype(F32)

    def p1_stages(c, h):
        rows = slice(c * cc, (c + 1) * cc)
        lanes = slice(h * dh, (h + 1) * dh)
        s = {}

        def s0():
            s["kq"] = lax.dot_general(kbq_s[2 * c * cc:(2 * c + 2) * cc, lanes], kn_s[rows, lanes], _NT,
                                      preferred_element_type=F32)

        def s1():
            pdiff = jnp.broadcast_to(gc_s[rows, h:h + 1], (cc, cc)) - gcr_s[h:h + 1, rows]
            decay = jnp.exp(jnp.where(row >= col, pdiff, -jnp.inf))
            kq = s.pop("kq")
            s["m"] = jnp.where(row > col, kq[:cc] * decay, 0.0)
            a_s[rows, h * cc:(h + 1) * cc] = (kq[cc:] * decay).astype(BF16)
            s["m1"] = s["m"].astype(BF16)
            s["p2"] = _dot_bf16(s["m1"], s["m1"])

        def s2():
            s["p2b"] = s["p2"].astype(BF16)
            s["p4"] = _dot_bf16(s["p2b"], s["p2b"])
            s["mp2"] = _dot_bf16(s.pop("m1"), s["p2b"])

        def s3():
            s["p4b"] = s["p4"].astype(BF16)
            s["p8"] = _dot_bf16(s["p4b"], s["p4b"])
            s["x1"] = eye - s.pop("m") + s.pop("p2") - s.pop("mp2")
            s.pop("p2b")

        def s4():
            p8b = s["p8"].astype(BF16)
            s["p16"] = _dot_bf16(p8b, p8b)
            s["p12"] = _dot_bf16(s.pop("p4b"), p8b)

        def s5():
            s["p16b"] = s["p16"].astype(BF16)
            s["p32"] = _dot_bf16(s["p16b"], s["p16b"])
            x2 = eye + s.pop("p4") + s.pop("p8") + s.pop("p12")
            s["x12"] = _dot_bf16(s.pop("x1"), x2)

        def s6():
            s["p48"] = _dot_bf16(s.pop("p16b"), s["p32"])

        def s7():
            x3 = eye + s.pop("p16") + s.pop("p32") + s.pop("p48")
            s["t"] = _dot_bf16(s.pop("x12"), x3)

        def s8():
            uw = _dot_bf16(s.pop("t"), vk_s[rows, 2 * h * dh:(2 * h + 2) * dh])
            u_s[rows, lanes] = uw[:, :dh]
            wqd_s[2 * c * cc:(2 * c + 1) * cc, lanes] = uw[:, dh:].astype(BF16)

        return (s0, s1, s2, s3, s4, s5, s6, s7, s8)

    chains = [(c, h) for c in range(n_chunks) for h in range(n_heads)]
    for g0 in range(0, len(chains), P1_GROUP):
        group = [p1_stages(c, h) for c, h in chains[g0:g0 + P1_GROUP]]
        for stage in zip(*group):
            for fn in stage:
                fn()

    for c in range(n_chunks):
        rows = slice(c * cc, (c + 1) * cc)
        states = [st_ref[h] for h in range(n_heads)]
        ws_qs = [lax.dot_general(wqd_s[2 * c * cc:(2 * c + 2) * cc, h * dh:(h + 1) * dh],
                                 states[h].astype(BF16), _NN, preferred_element_type=F32)
                 for h in range(n_heads)]
        v_new = [(u_s[rows, h * dh:(h + 1) * dh] - ws_qs[h][:cc]).astype(BF16) for h in range(n_heads)]
        o_intra = [lax.dot_general(a_s[rows, h * cc:(h + 1) * cc], v_new[h], _NN, preferred_element_type=F32)
                   for h in range(n_heads)]
        s_upd = [lax.dot_general(kd_s[rows, h * dh:(h + 1) * dh], v_new[h], _TN, preferred_element_type=F32)
                 for h in range(n_heads)]
        for h in range(n_heads):
            lanes = slice(h * dh, (h + 1) * dh)
            g_last = jnp.exp(jnp.broadcast_to(gc_s[(c + 1) * cc - 1:(c + 1) * cc, h:h + 1], (1, dh)))
            st_ref[h] = states[h] * g_last + s_upd[h]
            o = ws_qs[h][cc:] + o_intra[h]
            o = o * lax.rsqrt(jnp.mean(o * o, axis=-1, keepdims=True) + NORM_EPS) * nw_ref[...]
            o_ref[rows, lanes] = (o * _silu(z_ref[rows, lanes])).astype(o_ref.dtype)


def mix_gdn(sq, sk, sv, sz, sab, conv_w, a_log, dt_bias, norm_w, batch, seq, ts=256):
    m = sq[0].shape[0]
    c = conv_w.shape[1] // 3
    n_heads = c // GDN_HEAD_DIM
    nt = seq // ts
    r = ts // SUBLANE

    def main(colblk):
        return pl.BlockSpec((ts, c), lambda b, s: (b * nt + s, colblk))

    def halo(colblk):
        return pl.BlockSpec((SUBLANE, c), lambda b, s: (jnp.maximum((b * nt + s) * r - 1, 0), colblk))

    def wspec(j):
        return pl.BlockSpec((conv_w.shape[0], c), lambda b, s: (0, j))

    vec = pl.BlockSpec((1, LANE), lambda b, s: (0, 0))
    pad = LANE - n_heads
    alog = jnp.pad(a_log, (0, pad)).reshape(1, LANE)
    dtb = jnp.pad(dt_bias, (0, pad)).reshape(1, LANE)
    kern = functools.partial(_gdn_kernel, n_heads=n_heads)
    return pl.pallas_call(
        kern,
        grid=(batch, nt),
        in_specs=[main(sq[1]), main(sk[1]), main(sv[1]),
                  halo(sq[1]), halo(sk[1]), halo(sv[1]),
                  main(sz[1]),
                  pl.BlockSpec((ts, LANE), lambda b, s: (b * nt + s, sab[1])),
                  wspec(0), wspec(1), wspec(2), vec, vec, vec],
        out_specs=pl.BlockSpec((ts, c), lambda b, s: (b * nt + s, 0)),
        out_shape=jax.ShapeDtypeStruct((m, c), BF16),
        scratch_shapes=[pltpu.VMEM((n_heads, GDN_HEAD_DIM, GDN_HEAD_DIM), F32),
                        pltpu.VMEM((SUBLANE + ts, c), F32),
                        pltpu.VMEM((ts, c), F32),
                        pltpu.VMEM((ts, c), F32),
                        pltpu.VMEM((ts, c), F32),
                        pltpu.VMEM((ts, LANE), F32),
                        pltpu.VMEM((LANE, ts), F32),
                        pltpu.VMEM((ts, c), BF16),
                        pltpu.VMEM((2 * ts, c), BF16),
                        pltpu.VMEM((ts, 2 * c), BF16),
                        pltpu.VMEM((2 * ts, c), BF16),
                        pltpu.VMEM((ts, c), BF16),
                        pltpu.VMEM((ts, c), F32),
                        pltpu.VMEM((ts, n_heads * GDN_CHUNK), BF16)],
        compiler_params=_cparams(("arbitrary", "arbitrary")),
        name="mix_gdn",
    )(sq[0], sk[0], sv[0], sq[0], sk[0], sv[0], sz[0], sab[0],
      conv_w, conv_w, conv_w, alog, dtb, norm_w.reshape(1, LANE))


def _layer(x2, batch, seq, li, p, stacked):
    cg = p["sc_conv_w"].shape[1]
    n_heads = cg // GDN_HEAD_DIM
    n_main = 9 * cg
    n_ab = 2 * n_heads

    w_in = stacked["w_in"]
    w_main = cast_weight(w_in, li, n_main)
    w_tail = jnp.concatenate(
        [w_in[li, :, n_main + n_ab:], w_in[li, :, n_main:n_main + n_ab],
         jnp.zeros((w_in.shape[1], LANE - n_ab), w_in.dtype)], axis=1).astype(BF16)

    h = rmsnorm(x2, p["attn_norm_w"], BF16)
    proj = matmul(h, w_main, F32, tm=2048, tn=512)
    tail = matmul(h, w_tail, F32, tm=1024, tn=w_tail.shape[1])

    y_a = mix_sconv((proj, 0), (proj, 1), (proj, 2), p["sc_conv_w"], seq)
    y_b = mix_conformer((proj, 3), (proj, 4), p["cf_conv_w"], p["cf_conv_b"], p["cf_ln_w"], p["cf_ln_b"], seq)
    y_c = mix_gdn((proj, 5), (proj, 6), (proj, 7), (proj, 8), (tail, cg // LANE),
                  p["gdn_conv_w"], p["gdn_a_log"], p["gdn_dt_bias"], p["gdn_norm_w"], batch, seq)
    y_d = mix_pool((tail, 0), p["pool_w"], p["pool_scale"], seq)

    w_out = cast_weight(stacked["w_out"], li, stacked["w_out"].shape[2])
    x2 = matmul4_residual((y_a, y_b, y_c, y_d), w_out, x2, tm=1024, tn=512)

    d_ff = stacked["w_gate"].shape[2]
    wg = cast_weight(stacked["w_gate"], li, d_ff)
    wu = cast_weight(stacked["w_up"], li, d_ff)
    wd = cast_weight(stacked["w_down"], li, stacked["w_down"].shape[2])
    h = rmsnorm(x2, p["ffn_norm_w"], BF16)
    act = matmul_swiglu(h, wg, wu, tm=2048, tn=256)
    x2 = matmul_residual(act, wd, x2, tm=512, tn=512)
    return x2


def kernel(x, attn_norm_w, w_in, sc_conv_w, cf_conv_w, cf_conv_b, cf_ln_w, cf_ln_b, gdn_conv_w,
           gdn_a_log, gdn_dt_bias, gdn_norm_w, pool_w, pool_scale, w_out, ffn_norm_w, w_gate, w_up,
           w_down, final_norm_w):
    batch, seq, d = x.shape
    small = dict(attn_norm_w=attn_norm_w, sc_conv_w=sc_conv_w, cf_conv_w=cf_conv_w,
                 cf_conv_b=cf_conv_b, cf_ln_w=cf_ln_w, cf_ln_b=cf_ln_b, gdn_conv_w=gdn_conv_w,
                 gdn_a_log=gdn_a_log, gdn_dt_bias=gdn_dt_bias, gdn_norm_w=gdn_norm_w, pool_w=pool_w,
                 pool_scale=pool_scale, ffn_norm_w=ffn_norm_w)
    stacked = dict(w_in=w_in, w_out=w_out, w_gate=w_gate, w_up=w_up, w_down=w_down)
    x2 = x.reshape(batch * seq, d)
    for li in range(attn_norm_w.shape[0]):
        x2 = _layer(x2, batch, seq, li, {name: val[li] for name, val in small.items()}, stacked)
    out = rmsnorm(x2, final_norm_w, x.dtype)
    return out.reshape(batch, seq, d)
```

```python
import functools

import jax
import jax.numpy as jnp
from jax import lax
from jax.experimental import pallas as pl
from jax.experimental.pallas import tpu as pltpu

NORM_EPS = 1e-6
GDN_HEAD_DIM = 128
GDN_CHUNK = 64
POOL_WINDOWS = (2, 4, 8, 16)
LANE = 128
SUBLANE = 8
VMEM_LIMIT = 56 * 1024 * 1024

BF16 = jnp.bfloat16
F32 = jnp.float32


def _cparams(sem):
    return pltpu.CompilerParams(dimension_semantics=sem, vmem_limit_bytes=VMEM_LIMIT)


def _sigmoid(x):
    return 1.0 / (1.0 + jnp.exp(-x))


def _silu(x):
    return x * _sigmoid(x)


_NN = (((1,), (0,)), ((), ()))
_NT = (((1,), (1,)), ((), ()))
_TN = (((0,), (0,)), ((), ()))


def _dot_bf16(a, b, dims=_NN):
    return lax.dot_general(a.astype(BF16), b.astype(BF16), dims, preferred_element_type=F32)


def _cast_kernel(x_ref, o_ref, *, k_valid):
    tk, n = x_ref.shape
    r = pl.program_id(0)
    x = jnp.where(r * tk < k_valid, x_ref[...], 0.0).astype(o_ref.dtype)
    o_ref[:, :n] = x
    if o_ref.shape[1] > n:
        o_ref[:, n:] = jnp.zeros((tk, o_ref.shape[1] - n), o_ref.dtype)


def cast_weight(w_stacked, layer, k_out=None, n_out=None, tk=128):
    _, k, n = w_stacked.shape
    k_out = k if k_out is None else k_out
    n_out = n if n_out is None else n_out
    last = k // tk - 1
    return pl.pallas_call(
        functools.partial(_cast_kernel, k_valid=k),
        grid=(k_out // tk,),
        in_specs=[pl.BlockSpec((None, tk, n), lambda r: (layer, jnp.minimum(r, last), 0))],
        out_specs=pl.BlockSpec((tk, n_out), lambda r: (r, 0)),
        out_shape=jax.ShapeDtypeStruct((k_out, n_out), BF16),
        compiler_params=_cparams(("parallel",)),
        name="cast_weight",
    )(w_stacked)


IN_TILE = 512


def _cast_in_kernel(x_ref, o_ref, *, n_ab, ab_block):
    xt = x_ref[...].T
    col =lax.broadcasted_iota(jnp.int32, xt.shape, 1)
    keep = jnp.logical_or(pl.program_id(0) != ab_block, col < n_ab)
    o_ref[...] = jnp.where(keep, xt, 0.0).astype(o_ref.dtype)


def cast_in_weight(w_in_t, layer, n_main, n_ab, n_pool):
    _, n, k = w_in_t.shape
    main_blocks = n_main // IN_TILE
    pool_blocks = n_pool // IN_TILE
    ab_block = main_blocks + pool_blocks

    def src_row(r):
        return jnp.where(r < main_blocks, r * IN_TILE,
                         jnp.where(r < ab_block, n_main + n_ab + (r - main_blocks) * IN_TILE, n_main))

    assert n_main + IN_TILE <= n
    return pl.pallas_call(
        functools.partial(_cast_in_kernel, n_ab=n_ab, ab_block=ab_block),
        grid=(ab_block + 1,),
        in_specs=[pl.BlockSpec((pl.Element(IN_TILE), pl.Element(k)),
                               lambda r: (pl.multiple_of(layer * n + src_row(r), SUBLANE), 0))],
        out_specs=pl.BlockSpec((k, IN_TILE), lambda r: (0, r)),
        out_shape=jax.ShapeDtypeStruct((k, (ab_block + 1) * IN_TILE), BF16),
        compiler_params=_cparams(("parallel",)),
        name="cast_in_weight",
    )(w_in_t.reshape(-1, k))


def _rmsnorm_kernel(x_ref, w_ref, o_ref):
    x = x_ref[...]
    ms = jnp.mean(x * x, axis=-1, keepdims=True)
    o_ref[...] = (x * lax.rsqrt(ms + NORM_EPS) * w_ref[...]).astype(o_ref.dtype)


def rmsnorm(x, w, out_dtype, tm=256):
    m, d = x.shape
    return pl.pallas_call(
        _rmsnorm_kernel,
        grid=(m // tm,),
        in_specs=[pl.BlockSpec((tm, d), lambda i: (i, 0)),
                  pl.BlockSpec((1, d), lambda i: (0, 0))],
        out_specs=pl.BlockSpec((tm, d), lambda i: (i, 0)),
        out_shape=jax.ShapeDtypeStruct((m, d), out_dtype),
        compiler_params=_cparams(("parallel",)),
        name="rmsnorm",
    )(x, w.reshape(1, d))


def _panel_spec(tm, k):
    if tm > 1024:
        return pl.BlockSpec((tm, k), lambda i, j: (i, 0), pipeline_mode=pl.Buffered(1))
    return pl.BlockSpec((tm, k), lambda i, j: (i, 0))


def _mm_kernel(a_ref, b_ref, o_ref):
    o_ref[...] = jnp.dot(a_ref[...], b_ref[...], preferred_element_type=F32).astype(o_ref.dtype)


def matmul(a, b, out_dtype, tm, tn):
    m, k = a.shape
    n = b.shape[1]
    return pl.pallas_call(
        _mm_kernel,
        grid=(m // tm, n // tn),
        in_specs=[_panel_spec(tm, k),
                  pl.BlockSpec((k, tn), lambda i, j: (0, j))],
        out_specs=pl.BlockSpec((tm, tn), lambda i, j: (i, j)),
        out_shape=jax.ShapeDtypeStruct((m, n), out_dtype),
        compiler_params=_cparams(("parallel", "arbitrary")),
        name="matmul",
    )(a, b)


def _mm_res_kernel(a_ref, b_ref, r_ref, o_ref):
    o_ref[...] = r_ref[...] + jnp.dot(a_ref[...], b_ref[...], preferred_element_type=F32)


def matmul_residual(a, b, res, tm, tn):
    m, k = a.shape
    n = b.shape[1]
    return pl.pallas_call(
        _mm_res_kernel,
        grid=(m // tm, n // tn),
        in_specs=[pl.BlockSpec((tm, k), lambda i, j: (i, 0)),
                  pl.BlockSpec((k, tn), lambda i, j: (0, j)),
                  pl.BlockSpec((tm, tn), lambda i, j: (i, j))],
        out_specs=pl.BlockSpec((tm, tn), lambda i, j: (i, j)),
        out_shape=jax.ShapeDtypeStruct((m, n), F32),
        compiler_params=_cparams(("parallel", "arbitrary")),
        name="matmul_residual",
    )(a, b, res)


def _mm4_res_kernel(a0, a1, a2, a3, b_ref, r_ref, o_ref):
    kg = a0.shape[1]
    acc = r_ref[...]
    for g, a in enumerate((a0, a1, a2, a3)):
        acc = acc + jnp.dot(a[...], b_ref[g * kg:(g + 1) * kg, :], preferred_element_type=F32)
    o_ref[...] = acc


def matmul4_residual(parts, b, res, tm, tn):
    m, kg = parts[0].shape
    k, n = b.shape
    a_spec = pl.BlockSpec((tm, kg), lambda i, j: (i, 0))
    return pl.pallas_call(
        _mm4_res_kernel,
        grid=(m // tm, n // tn),
        in_specs=[a_spec, a_spec, a_spec, a_spec,
                  pl.BlockSpec((k, tn), lambda i, j: (0, j)),
                  pl.BlockSpec((tm, tn), lambda i, j: (i, j))],
        out_specs=pl.BlockSpec((tm, tn), lambda i, j: (i, j)),
        out_shape=jax.ShapeDtypeStruct((m, n), F32),
        compiler_params=_cparams(("parallel", "arbitrary")),
        name="out_proj",
    )(*parts, b, res)


def _mm_glu_kernel(a_ref, bg_ref, bu_ref, o_ref):
    a = a_ref[...]
    g = jnp.dot(a, bg_ref[...], preferred_element_type=F32)
    u = jnp.dot(a, bu_ref[...], preferred_element_type=F32)
    o_ref[...] = (_silu(g) * u).astype(o_ref.dtype)


def matmul_swiglu(a, bg, bu, tm, tn):
    m, k = a.shape
    n = bg.shape[1]
    b_spec = pl.BlockSpec((k, tn), lambda i, j: (0, j))
    return pl.pallas_call(
        _mm_glu_kernel,
        grid=(m // tm, n // tn),
        in_specs=[_panel_spec(tm, k), b_spec, b_spec],
        out_specs=pl.BlockSpec((tm, tn), lambda i, j: (i, j)),
        out_shape=jax.ShapeDtypeStruct((m, n), BF16),
        compiler_params=_cparams(("parallel", "arbitrary")),
        name="ffn_gate_up",
    )(a, bg, bu)


def _halo_spec(ts, hb, c, col):
    r = ts // hb
    return pl.BlockSpec((hb, c), lambda i: (jnp.maximum(i * r - 1, 0), col))


def _main_spec(ts, c, col):
    return pl.BlockSpec((ts, c), lambda i: (i, col))


def _fill_history(scr, first, halo, cur, hb):
    scr[0:hb, :] = jnp.where(first, 0.0, halo)
    scr[hb:, :] = cur


def _mix_sconv_kernel(b_ref, c_ref, h_ref, ch_ref, hh_ref, w_ref, o_ref, scr, *, tiles_per_seq):
    ts = o_ref.shape[0]
    first = (pl.program_id(0) % tiles_per_seq) == 0
    m = c_ref[...] * h_ref[...]
    _fill_history(scr, first, ch_ref[...] * hh_ref[...], m, SUBLANE)
    w = w_ref[...]
    kw = w.shape[0]
    y = w[kw - 1:kw, :] * m
    for d in range(1, kw):
        y = y + w[kw - 1 - d:kw - d, :] * scr[SUBLANE - d:SUBLANE - d + ts, :]
    o_ref[...] = (b_ref[...] * y).astype(o_ref.dtype)


def mix_sconv(sb, sc, sh, w, seq, ts=256):
    m = sb[0].shape[0]
    c = w.shape[1]
    kern = functools.partial(_mix_sconv_kernel, tiles_per_seq=seq // ts)
    return pl.pallas_call(
        kern,
        grid=(m // ts,),
        in_specs=[_main_spec(ts, c, sb[1]), _main_spec(ts, c, sc[1]), _main_spec(ts, c, sh[1]),
                  _halo_spec(ts, SUBLANE, c, sc[1]), _halo_spec(ts, SUBLANE, c, sh[1]),
                  pl.BlockSpec(w.shape, lambda i: (0, 0))],
        out_specs=pl.BlockSpec((ts, c), lambda i: (i, 0)),
        out_shape=jax.ShapeDtypeStruct((m, c), BF16),
        scratch_shapes=[pltpu.VMEM((SUBLANE + ts, c), F32)],
        compiler_params=_cparams(("parallel",)),
        name="mix_sconv",
    )(sb[0], sc[0], sh[0], sc[0], sh[0], w)


CF_HALO = 32


def _mix_conformer_kernel(v_ref, g_ref, vh_ref, gh_ref, w_ref, cb_ref, lw_ref, lb_ref, o_ref,
                          scr, sh_scr, cf_scr, *, tiles_per_seq):
    ts, c = o_ref.shape
    first = (pl.program_id(0) % tiles_per_seq) == 0
    glu = v_ref[...] * _sigmoid(g_ref[...])
    _fill_history(scr, first, vh_ref[...] * _sigmoid(gh_ref[...]), glu, CF_HALO)
    n_sh = sh_scr.shape[1]
    for r in range(1, SUBLANE):
        sh_scr[r - 1] = scr[SUBLANE - r:SUBLANE - r + n_sh, :]
    kw = w_ref.shape[0]
    s1 = jnp.zeros((ts, LANE), F32)
    for c0 in range(0, c, LANE):
        acc = jnp.zeros((ts, LANE), F32) + cb_ref[:, c0:c0 + LANE]
        for d in range(kw):
            a, r = divmod(d, SUBLANE)
            if r == 0:
                tap = scr[CF_HALO - d:CF_HALO - d + ts, c0:c0 + LANE]
            else:
                off = CF_HALO - SUBLANE * (a + 1)
                tap = sh_scr[r - 1, off:off + ts, c0:c0 + LANE]
            acc = acc + w_ref[kw - 1 - d:kw - d, c0:c0 + LANE] * tap
        cf_scr[:, c0:c0 + LANE] = acc
        s1 = s1 + acc
    mu = jnp.sum(s1, axis=-1, keepdims=True) * (1.0 / c)
    s2 = jnp.zeros((ts, LANE), F32)
    for c0 in range(0, c, LANE):
        xc = cf_scr[:, c0:c0 + LANE] - mu
        s2 = s2 + xc * xc
    rstd = lax.rsqrt(jnp.sum(s2, axis=-1, keepdims=True) * (1.0 / c) + NORM_EPS)
    for c0 in range(0, c, LANE):
        y = (cf_scr[:, c0:c0 + LANE] - mu) * rstd * lw_ref[:, c0:c0 + LANE] + lb_ref[:, c0:c0 + LANE]
        o_ref[:, c0:c0 + LANE] = _silu(y).astype(o_ref.dtype)


def mix_conformer(sv, sg, w, cb, lw, lb, seq, ts=128):
    m = sv[0].shape[0]
    c = w.shape[1]
    kern = functools.partial(_mix_conformer_kernel, tiles_per_seq=seq // ts)
    vec = pl.BlockSpec((1, c), lambda i: (0, 0))
    return pl.pallas_call(
        kern,
        grid=(m // ts,),
        in_specs=[_main_spec(ts, c, sv[1]), _main_spec(ts, c, sg[1]),
                  _halo_spec(ts, CF_HALO, c, sv[1]), _halo_spec(ts, CF_HALO, c, sg[1]),
                  pl.BlockSpec(w.shape, lambda i: (0, 0)), vec, vec, vec],
        out_specs=pl.BlockSpec((ts, c), lambda i: (i, 0)),
        out_shape=jax.ShapeDtypeStruct((m, c), BF16),
        scratch_shapes=[pltpu.VMEM((CF_HALO + ts, c), F32),
                        pltpu.VMEM((SUBLANE - 1, CF_HALO - SUBLANE + ts, c), F32),
                        pltpu.VMEM((ts, c), F32)],
        compiler_params=_cparams(("parallel",)),
        name="mix_conformer",
    )(sv[0], sg[0], sv[0], sg[0], w, cb.reshape(1, c), lw.reshape(1, c), lb.reshape(1, c))


POOL_HALO = 16


def _mix_pool_kernel(u_ref, uh_ref, pw_ref, ps_ref, o_ref, scr, *, tiles_per_seq):
    ts, c = o_ref.shape
    ng = pw_ref.shape[0]
    cg = c // ng
    tile = pl.program_id(0) % tiles_per_seq
    first = tile == 0
    _fill_history(scr, first, uh_ref[...], u_ref[...], POOL_HALO)
    pos = tile * ts + lax.broadcasted_iota(jnp.int32, (ts, 1), 0)
    for gi, win in enumerate(POOL_WINDOWS):
        c0 = gi * cg
        cur = scr[POOL_HALO:POOL_HALO + ts, c0:c0 + cg]
        acc = cur
        for d in range(1, win):
            acc = acc + scr[POOL_HALO - d:POOL_HALO - d + ts, c0:c0 + cg]
        cnt = jnp.minimum(pos + 1, win).astype(F32)
        p = acc / cnt - cur
        y = _dot_bf16(p, pw_ref[gi])
        o_ref[:, c0:c0 + cg] = (y * ps_ref[:, c0:c0 + cg]).astype(o_ref.dtype)


def mix_pool(su, pool_w, pool_scale, seq, ts=256):
    m = su[0].shape[0]
    ng, cg, _ = pool_w.shape
    c = ng * cg
    kern = functools.partial(_mix_pool_kernel, tiles_per_seq=seq // ts)
    return pl.pallas_call(
        kern,
        grid=(m // ts,),
        in_specs=[_main_spec(ts, c, su[1]), _halo_spec(ts, POOL_HALO, c, su[1]),
                  pl.BlockSpec(pool_w.shape, lambda i: (0, 0, 0)),
                  pl.BlockSpec((1, c), lambda i: (0, 0))],
        out_specs=pl.BlockSpec((ts, c), lambda i: (i, 0)),
        out_shape=jax.ShapeDtypeStruct((m, c), BF16),
        scratch_shapes=[pltpu.VMEM((POOL_HALO + ts, c), F32)],
        compiler_params=_cparams(("parallel",)),
        name="mix_pool",
    )(su[0], su[0], pool_w.astype(BF16), pool_scale.reshape(1, c))


def _split3(x):
    x1 = x.astype(BF16)
    r1 = x - x1.astype(F32)
    x2 = r1.astype(BF16)
    x3 = (r1 - x2.astype(F32)).astype(BF16)
    return x1, x2, x3


P1_GROUP = 16


def _gdn_kernel(q_ref, k_ref, v_ref, qh_ref, kh_ref, vh_ref, z_ref, ab_ref,
                wq_ref, wk_ref, wv_ref, alog_ref, dtb_ref, nw_ref, o_ref,
                st_ref, hist, qs, ks, vs, gc_s, gcr_s, kn_s, kbq_s, vk_s, wqd_s, kd_s, u_s, a_s,
                *, n_heads):
    ts = o_ref.shape[0]
    dh = GDN_HEAD_DIM
    cc = GDN_CHUNK
    n_chunks = ts // cc
    first = pl.program_id(1) == 0

    @pl.when(first)
    def _():
        st_ref[...] = jnp.zeros_like(st_ref)

    def conv_silu(x_ref, xh_ref, w_ref, dst):
        _fill_history(hist, first, xh_ref[...], x_ref[...], SUBLANE)
        kw = w_ref.shape[0]
        acc = w_ref[kw - 1:kw, :] * x_ref[...]
        for d in range(1, kw):
            acc = acc + w_ref[kw - 1 - d:kw - d, :] * hist[SUBLANE - d:SUBLANE - d + ts, :]
        dst[...] = _silu(acc)

    conv_silu(q_ref, qh_ref, wq_ref, qs)
    conv_silu(k_ref, kh_ref, wk_ref, ks)
    conv_silu(v_ref, vh_ref, wv_ref, vs)

    ab = ab_ref[...]
    xg = ab + dtb_ref[...]
    softplus = jnp.maximum(xg, 0.0) + jnp.log1p(jnp.exp(-jnp.abs(xg)))
    g_all = -jnp.exp(alog_ref[...]) * softplus
    beta_all = _sigmoid(ab)

    rt = lax.broadcasted_iota(jnp.int32, (ts, ts), 0)
    ct = lax.broadcasted_iota(jnp.int32, (ts, ts), 1)
    tri_blk = jnp.where((rt >= ct) & ((rt // cc) == (ct // cc)), 1.0, 0.0).astype(BF16)
    g1, g2, g3 = _split3(g_all)
    gc_all = (jnp.dot(tri_blk, g1, preferred_element_type=F32)
              + jnp.dot(tri_blk, g2, preferred_element_type=F32)
              + jnp.dot(tri_blk, g3, preferred_element_type=F32))
    gc_s[...] = gc_all
    gcr_s[...] = gc_all.T

    scale = dh ** -0.5

    for h in range(n_heads):
        lanes = slice(h * dh, (h + 1) * dh)
        gc_col = jnp.broadcast_to(gc_all[:, h:h + 1], (ts, dh))
        b_col = jnp.broadcast_to(beta_all[:, n_heads + h:n_heads + h + 1], (ts, dh))
        q_all = qs[:, lanes]
        k_all = ks[:, lanes]
        q_all = q_all * (lax.rsqrt(jnp.sum(q_all * q_all, axis=-1, keepdims=True) + NORM_EPS) * scale)
        k_all = k_all * lax.rsqrt(jnp.sum(k_all * k_all, axis=-1, keepdims=True) + NORM_EPS)
        egc_all = jnp.exp(gc_col)
        kb_all = k_all * b_col
        kn_s[:, lanes] = k_all.astype(BF16)
        vk_s[:, 2 * h * dh:(2 * h + 1) * dh] = (vs[:, lanes] * b_col).astype(BF16)
        vk_s[:, (2 * h + 1) * dh:(2 * h + 2) * dh] = (kb_all * egc_all).astype(BF16)
        qd_all = q_all * egc_all
        for c in range(n_chunks):
            rows = slice(c * cc, (c + 1) * cc)
            gc_last = gc_col[(c + 1) * cc - 1:(c + 1) * cc, :]
            kbq_s[2 * c * cc:(2 * c + 1) * cc, lanes] = kb_all[rows].astype(BF16)
            kbq_s[(2 * c + 1) * cc:(2 * c + 2) * cc, lanes] = q_all[rows].astype(BF16)
            wqd_s[(2 * c + 1) * cc:(2 * c + 2) * cc, lanes] = qd_all[rows].astype(BF16)
            kd_s[rows, lanes] = (k_all[rows] * jnp.exp(gc_last - gc_col[rows])).astype(BF16)

    row = lax.broadcasted_iota(jnp.int32, (cc, cc), 0)
    col = lax.broadcasted_iota(jnp.int32, (cc, cc), 1)
    eye = (row == col).astype(F32)

    def p1_stages(c, h):
        rows = slice(c * cc, (c + 1) * cc)
        lanes = slice(h * dh, (h + 1) * dh)
        s = {}

        def s0():
            s["kq"] = lax.dot_general(kbq_s[2 * c * cc:(2 * c + 2) * cc, lanes], kn_s[rows, lanes], _NT,
                                      preferred_element_type=F32)

        def s1():
            pdiff = jnp.broadcast_to(gc_s[rows, h:h + 1], (cc, cc)) - gcr_s[h:h + 1, rows]
            decay = jnp.exp(jnp.where(row >= col, pdiff, -jnp.inf))
            kq = s.pop("kq")
            s["m"] = jnp.where(row > col, kq[:cc] * decay, 0.0)
            a_s[rows, h * cc:(h + 1) * cc] = (kq[cc:] * decay).astype(BF16)
            s["m1"] = s["m"].astype(BF16)
            s["p2"] = _dot_bf16(s["m1"], s["m1"])

        def s2():
            s["p2b"] = s["p2"].astype(BF16)
            s["p4"] = _dot_bf16(s["p2b"], s["p2b"])
            s["mp2"] = _dot_bf16(s.pop("m1"), s["p2b"])

        def s3():
            s["p4b"] = s["p4"].astype(BF16)
            s["p8"] = _dot_bf16(s["p4b"], s["p4b"])
            s["x1"] = eye - s.pop("m") + s.pop("p2") - s.pop("mp2")
            s.pop("p2b")

        def s4():
            p8b = s["p8"].astype(BF16)
            s["p16"] = _dot_bf16(p8b, p8b)
            s["p12"] = _dot_bf16(s.pop("p4b"), p8b)

        def s5():
            s["p16b"] = s["p16"].astype(BF16)
            s["p32"] = _dot_bf16(s["p16b"], s["p16b"])
            x2 = eye + s.pop("p4") + s.pop("p8") + s.pop("p12")
            s["x12"] = _dot_bf16(s.pop("x1"), x2)

        def s6():
            s["p48"] = _dot_bf16(s.pop("p16b"), s["p32"])

        def s7():
            x3 = eye + s.pop("p16") + s.pop("p32") + s.pop("p48")
            s["t"] = _dot_bf16(s.pop("x12"), x3)

        def s8():
            uw = _dot_bf16(s.pop("t"), vk_s[rows, 2 * h * dh:(2 * h + 2) * dh])
            u_s[rows, lanes] = uw[:, :dh]
            wqd_s[2 * c * cc:(2 * c + 1) * cc, lanes] = uw[:, dh:].astype(BF16)

        return (s0, s1, s2, s3, s4, s5, s6, s7, s8)

    chains = [(c, h) for c in range(n_chunks) for h in range(n_heads)]
    for g0 in range(0, len(chains), P1_GROUP):
        group = [p1_stages(c, h) for c, h in chains[g0:g0 + P1_GROUP]]
        for stage in zip(*group):
            for fn in stage:
                fn()

    for c in range(n_chunks):
        rows = slice(c * cc, (c + 1) * cc)
        states = [st_ref[h] for h in range(n_heads)]
        ws_qs = [lax.dot_general(wqd_s[2 * c * cc:(2 * c + 2) * cc, h * dh:(h + 1) * dh],
                                 states[h].astype(BF16), _NN, preferred_element_type=F32)
                 for h in range(n_heads)]
        v_new = [(u_s[rows, h * dh:(h + 1) * dh] - ws_qs[h][:cc]).astype(BF16) for h in range(n_heads)]
        o_intra = [lax.dot_general(a_s[rows, h * cc:(h + 1) * cc], v_new[h], _NN, preferred_element_type=F32)
                   for h in range(n_heads)]
        s_upd = [lax.dot_general(kd_s[rows, h * dh:(h + 1) * dh], v_new[h], _TN, preferred_element_type=F32)
                 for h in range(n_heads)]
        for h in range(n_heads):
            lanes = slice(h * dh, (h + 1) * dh)
            g_last = jnp.exp(jnp.broadcast_to(gc_s[(c + 1) * cc - 1:(c + 1) * cc, h:h + 1], (1, dh)))
            st_ref[h] = states[h] * g_last + s_upd[h]
            o = ws_qs[h][cc:] + o_intra[h]
            o = o * lax.rsqrt(jnp.mean(o * o, axis=-1, keepdims=True) + NORM_EPS) * nw_ref[...]
            o_ref[rows, lanes] = (o * _silu(z_ref[rows, lanes])).astype(o_ref.dtype)


def mix_gdn(sq, sk, sv, sz, sab, conv_w, a_log, dt_bias, norm_w, batch, seq, ts=256):
    m = sq[0].shape[0]
    c = conv_w.shape[1] // 3
    n_heads = c // GDN_HEAD_DIM
    nt = seq // ts
    r = ts // SUBLANE

    def main(colblk):
        return pl.BlockSpec((ts, c), lambda b, s: (b * nt + s, colblk))

    def halo(colblk):
        return pl.BlockSpec((SUBLANE, c), lambda b, s: (jnp.maximum((b * nt + s) * r - 1, 0), colblk))

    def wspec(j):
        return pl.BlockSpec((conv_w.shape[0], c), lambda b, s: (0, j))

    vec = pl.BlockSpec((1, LANE), lambda b, s: (0, 0))
    pad = LANE - n_heads
    alog = jnp.pad(a_log, (0, pad)).reshape(1, LANE)
    dtb = jnp.pad(dt_bias, (0, pad)).reshape(1, LANE)
    kern = functools.partial(_gdn_kernel, n_heads=n_heads)
    return pl.pallas_call(
        kern,
        grid=(batch, nt),
        in_specs=[main(sq[1]), main(sk[1]), main(sv[1]),
                  halo(sq[1]), halo(sk[1]), halo(sv[1]),
                  main(sz[1]),
                  pl.BlockSpec((ts, LANE), lambda b, s: (b * nt + s, sab[1])),
                  wspec(0), wspec(1), wspec(2), vec, vec, vec],
        out_specs=pl.BlockSpec((ts, c), lambda b, s: (b * nt + s, 0)),
        out_shape=jax.ShapeDtypeStruct((m, c), BF16),
        scratch_shapes=[pltpu.VMEM((n_heads, GDN_HEAD_DIM, GDN_HEAD_DIM), F32),
                        pltpu.VMEM((SUBLANE + ts, c), F32),
                        pltpu.VMEM((ts, c), F32),
                        pltpu.VMEM((ts, c), F32),
                        pltpu.VMEM((ts, c), F32),
                        pltpu.VMEM((ts, LANE), F32),
                        pltpu.VMEM((LANE, ts), F32),
                        pltpu.VMEM((ts, c), BF16),
                        pltpu.VMEM((2 * ts, c), BF16),
                        pltpu.VMEM((ts, 2 * c), BF16),
                        pltpu.VMEM((2 * ts, c), BF16),
                        pltpu.VMEM((ts, c), BF16),
                        pltpu.VMEM((ts, c), F32),
                        pltpu.VMEM((ts, n_heads * GDN_CHUNK), BF16)],
        compiler_params=_cparams(("arbitrary", "arbitrary")),
        name="mix_gdn",
    )(sq[0], sk[0], sv[0], sq[0], sk[0], sv[0], sz[0], sab[0],
      conv_w, conv_w, conv_w, alog, dtb, norm_w.reshape(1, LANE))


FF_TILE = 512


def _layer(x2, batch, seq, li, p, stacked):
    cg = p["sc_conv_w"].shape[1]
    n_heads = cg // GDN_HEAD_DIM
    n_main = 9 * cg
    n_ab = 2 * n_heads

    w_all = cast_in_weight(stacked["w_in_t"], li, n_main, n_ab, cg)
    h = rmsnorm(x2, p["attn_norm_w"], BF16)
    proj = matmul(h, w_all, F32, tm=2048, tn=IN_TILE)

    y_a = mix_sconv((proj, 0), (proj, 1), (proj, 2), p["sc_conv_w"], seq)
    y_b = mix_conformer((proj, 3), (proj, 4), p["cf_conv_w"], p["cf_conv_b"], p["cf_ln_w"], p["cf_ln_b"], seq)
    y_c = mix_gdn((proj, 5), (proj, 6), (proj, 7), (proj, 8), (proj, (n_main + cg) // LANE),
                  p["gdn_conv_w"], p["gdn_a_log"], p["gdn_dt_bias"], p["gdn_norm_w"], batch, seq)
    y_d = mix_pool((proj, 9), p["pool_w"], p["pool_scale"], seq)

    w_out = cast_weight(stacked["w_out"], li)
    x2 = matmul4_residual((y_a, y_b, y_c, y_d), w_out, x2, tm=1024, tn=512)

    d_ff = stacked["w_gate"].shape[2]
    ff_pad = -(-d_ff // FF_TILE) * FF_TILE
    wg = cast_weight(stacked["w_gate"], li, n_out=ff_pad)
    wu = cast_weight(stacked["w_up"], li, n_out=ff_pad)
    wd = cast_weight(stacked["w_down"], li, k_out=ff_pad)
    h = rmsnorm(x2, p["ffn_norm_w"], BF16)
    act = matmul_swiglu(h, wg, wu, tm=1024, tn=FF_TILE)
    x2 = matmul_residual(act, wd, x2, tm=512, tn=512)
    return x2


def kernel(x, attn_norm_w, w_in, sc_conv_w, cf_conv_w, cf_conv_b, cf_ln_w, cf_ln_b, gdn_conv_w,
           gdn_a_log, gdn_dt_bias, gdn_norm_w, pool_w, pool_scale, w_out, ffn_norm_w, w_gate, w_up,
           w_down, final_norm_w):
    batch, seq, d = x.shape
    small = dict(attn_norm_w=attn_norm_w, sc_conv_w=sc_conv_w, cf_conv_w=cf_conv_w,
                 cf_conv_b=cf_conv_b, cf_ln_w=cf_ln_w, cf_ln_b=cf_ln_b, gdn_conv_w=gdn_conv_w,
                 gdn_a_log=gdn_a_log, gdn_dt_bias=gdn_dt_bias, gdn_norm_w=gdn_norm_w, pool_w=pool_w,
                 pool_scale=pool_scale, ffn_norm_w=ffn_norm_w)
    stacked = dict(w_in_t=jnp.swapaxes(w_in, 1, 2), w_out=w_out, w_gate=w_gate, w_up=w_up, w_down=w_down)
    x2 = x.reshape(batch * seq, d)
    for li in range(attn_norm_w.shape[0]):
        x2 = _layer(x2, batch, seq, li, {name: val[li] for name, val in small.items()}, stacked)
    out = rmsnorm(x2, final_norm_w, x.dtype)
    return out.reshape(batch, seq, d)
```

```python
import functools

import jax
import jax.numpy as jnp
from jax import lax
from jax.experimental import pallas as pl
from jax.experimental.pallas import tpu as pltpu

NORM_EPS = 1e-6
GDN_HEAD_DIM = 128
GDN_CHUNK = 64
POOL_WINDOWS = (2, 4, 8, 16)
LANE = 128
SUBLANE = 8
VMEM_LIMIT = 56 * 1024 * 1024

BF16 = jnp.bfloat16
F32 = jnp.float32


def _cparams(sem):
    return pltpu.CompilerParams(dimension_semantics=sem, vmem_limit_bytes=VMEM_LIMIT)


def _sigmoid(x):
    return 1.0 / (1.0 + jnp.exp(-x))


def _silu(x):
    return x * _sigmoid(x)


_NN = (((1,), (0,)), ((), ()))
_NT = (((1,), (1,)), ((), ()))
_TN = (((0,), (0,)), ((), ()))


def _dot_bf16(a, b, dims=_NN):
    return lax.dot_general(a.astype(BF16), b.astype(BF16), dims, preferred_element_type=F32)


def _cast_kernel(x_ref, o_ref):
    o_ref[...] = x_ref[...].astype(o_ref.dtype)


def cast_weight(w_stacked, layer, tk=128):
    _, k, n = w_stacked.shape
    return pl.pallas_call(
        _cast_kernel,
        grid=(k // tk,),
        in_specs=[pl.BlockSpec((None, tk, n), lambda r: (layer, r, 0))],
        out_specs=pl.BlockSpec((tk, n), lambda r: (r, 0)),
        out_shape=jax.ShapeDtypeStruct((k, n), BF16),
        compiler_params=_cparams(("parallel",)),
        name="cast_weight",
    )(w_stacked)


IN_TILE = 512


def _rmsnorm_kernel(x_ref, w_ref, o_ref):
    x = x_ref[...]
    ms = jnp.mean(x * x, axis=-1, keepdims=True)
    o_ref[...] = (x * lax.rsqrt(ms + NORM_EPS) * w_ref[...]).astype(o_ref.dtype)


def rmsnorm(x, w, out_dtype, tm=256):
    m, d = x.shape
    return pl.pallas_call(
        _rmsnorm_kernel,
        grid=(m // tm,),
        in_specs=[pl.BlockSpec((tm, d), lambda i: (i, 0)),
                  pl.BlockSpec((1, d), lambda i: (0, 0))],
        out_specs=pl.BlockSpec((tm, d), lambda i: (i, 0)),
        out_shape=jax.ShapeDtypeStruct((m, d), out_dtype),
        compiler_params=_cparams(("parallel",)),
        name="rmsnorm",
    )(x, w.reshape(1, d))


def _panel_spec(tm, k):
    if tm > 1024:
        return pl.BlockSpec((tm, k), lambda i, j: (i, 0), pipeline_mode=pl.Buffered(1))
    return pl.BlockSpec((tm, k), lambda i, j: (i, 0))


def _mm_in_kernel(a_ref, bt_ref, o_ref):
    o_ref[...] = lax.dot_general(a_ref[...], bt_ref[...].astype(BF16), _NT, preferred_element_type=F32)


def matmul_in_proj(a, w_in_t, layer, n_main, n_ab, n_pool, tm):
    m, k = a.shape
    _, n, _ = w_in_t.shape
    main_blocks = n_main // IN_TILE
    ab_block = main_blocks + n_pool // IN_TILE

    def src_row(j):
        return jnp.where(j < main_blocks, j * IN_TILE,
                         jnp.where(j < ab_block, n_main + n_ab + (j - main_blocks) * IN_TILE, n_main))

    assert n_main + IN_TILE <= n
    return pl.pallas_call(
        _mm_in_kernel,
        grid=(m // tm, ab_block + 1),
        in_specs=[_panel_spec(tm, k),
                  pl.BlockSpec((pl.Element(IN_TILE), pl.Element(k)),
                               lambda i, j: (pl.multiple_of(layer * n + src_row(j), SUBLANE), 0))],
        out_specs=pl.BlockSpec((tm, IN_TILE), lambda i, j: (i, j)),
        out_shape=jax.ShapeDtypeStruct((m, (ab_block + 1) * IN_TILE), F32),
        compiler_params=_cparams(("parallel", "arbitrary")),
        name="in_proj",
    )(a, w_in_t.reshape(-1, k))


def _mm_res_kernel(a_ref, b_ref, r_ref, o_ref):
    o_ref[...] = r_ref[...] + jnp.dot(a_ref[...], b_ref[...], preferred_element_type=F32)


def matmul_residual(a, b, res, tm, tn):
    m, k = a.shape
    n = b.shape[1]
    return pl.pallas_call(
        _mm_res_kernel,
        grid=(m // tm, n // tn),
        in_specs=[pl.BlockSpec((tm, k), lambda i, j: (i, 0)),
                  pl.BlockSpec((k, tn), lambda i, j: (0, j)),
                  pl.BlockSpec((tm, tn), lambda i, j: (i, j))],
        out_specs=pl.BlockSpec((tm, tn), lambda i, j: (i, j)),
        out_shape=jax.ShapeDtypeStruct((m, n), F32),
        compiler_params=_cparams(("parallel", "arbitrary")),
        name="matmul_residual",
    )(a, b, res)


def _layer_tile_spec(k, tn, layer):
    return pl.BlockSpec((None, k, tn), lambda i, j: (layer, 0, j))


def _mm4_res_kernel(a0, a1, a2, a3, b_ref, r_ref, o_ref):
    kg = a0.shape[1]
    acc = r_ref[...]
    for g, a in enumerate((a0, a1, a2, a3)):
        acc = acc + jnp.dot(a[...], b_ref[g * kg:(g + 1) * kg, :].astype(BF16), preferred_element_type=F32)
    o_ref[...] = acc


def matmul4_residual(parts, w_stacked, layer, res, tm, tn):
    m, kg = parts[0].shape
    _, k, n = w_stacked.shape
    a_spec = pl.BlockSpec((tm, kg), lambda i, j: (i, 0))
    return pl.pallas_call(
        _mm4_res_kernel,
        grid=(m // tm, n // tn),
        in_specs=[a_spec, a_spec, a_spec, a_spec,
                  _layer_tile_spec(k, tn, layer),
                  pl.BlockSpec((tm, tn), lambda i, j: (i, j))],
        out_specs=pl.BlockSpec((tm, tn), lambda i, j: (i, j)),
        out_shape=jax.ShapeDtypeStruct((m, n), F32),
        compiler_params=_cparams(("parallel", "arbitrary")),
        name="out_proj",
    )(*parts, w_stacked, res)


def _mm_glu_kernel(a_ref, bg_ref, bu_ref, o_ref):
    a = a_ref[...]
    g = jnp.dot(a, bg_ref[...].astype(BF16), preferred_element_type=F32)
    u = jnp.dot(a, bu_ref[...].astype(BF16), preferred_element_type=F32)
    o_ref[...] = (_silu(g) * u).astype(o_ref.dtype)


def matmul_swiglu(a, wg_stacked, wu_stacked, layer, tm, tn):
    m, k = a.shape
    n = wg_stacked.shape[2]
    b_spec = _layer_tile_spec(k, tn, layer)
    return pl.pallas_call(
        _mm_glu_kernel,
        grid=(m // tm, n // tn),
        in_specs=[_panel_spec(tm, k), b_spec, b_spec],
        out_specs=pl.BlockSpec((tm, tn), lambda i, j: (i, j)),
        out_shape=jax.ShapeDtypeStruct((m, n), BF16),
        compiler_params=_cparams(("parallel", "arbitrary")),
        name="ffn_gate_up",
    )(a, wg_stacked, wu_stacked)


def _halo_spec(ts, hb, c, col):
    r = ts // hb
    return pl.BlockSpec((hb, c), lambda i: (jnp.maximum(i * r - 1, 0), col))


def _main_spec(ts, c, col):
    return pl.BlockSpec((ts, c), lambda i: (i, col))


def _fill_history(scr, first, halo, cur, hb):
    scr[0:hb, :] = jnp.where(first, 0.0, halo)
    scr[hb:, :] = cur


def _mix_sconv_kernel(b_ref, c_ref, h_ref, ch_ref, hh_ref, w_ref, o_ref, scr, *, tiles_per_seq):
    ts = o_ref.shape[0]
    first = (pl.program_id(0) % tiles_per_seq) == 0
    m = c_ref[...] * h_ref[...]
    _fill_history(scr, first, ch_ref[...] * hh_ref[...], m, SUBLANE)
    w = w_ref[...]
    kw = w.shape[0]
    y = w[kw - 1:kw, :] * m
    for d in range(1, kw):
        y = y + w[kw - 1 - d:kw - d, :] * scr[SUBLANE - d:SUBLANE - d + ts, :]
    o_ref[...] = (b_ref[...] * y).astype(o_ref.dtype)


def mix_sconv(sb, sc, sh, w, seq, ts=256):
    m = sb[0].shape[0]
    c = w.shape[1]
    kern = functools.partial(_mix_sconv_kernel, tiles_per_seq=seq // ts)
    return pl.pallas_call(
        kern,
        grid=(m // ts,),
        in_specs=[_main_spec(ts, c, sb[1]), _main_spec(ts, c, sc[1]), _main_spec(ts, c, sh[1]),
                  _halo_spec(ts, SUBLANE, c, sc[1]), _halo_spec(ts, SUBLANE, c, sh[1]),
                  pl.BlockSpec(w.shape, lambda i: (0, 0))],
        out_specs=pl.BlockSpec((ts, c), lambda i: (i, 0)),
        out_shape=jax.ShapeDtypeStruct((m, c), BF16),
        scratch_shapes=[pltpu.VMEM((SUBLANE + ts, c), F32)],
        compiler_params=_cparams(("parallel",)),
        name="mix_sconv",
    )(sb[0], sc[0], sh[0], sc[0], sh[0], w)


CF_HALO = 32


def _mix_conformer_kernel(v_ref, g_ref, vh_ref, gh_ref, w_ref, cb_ref, lw_ref, lb_ref, o_ref,
                          scr, sh_scr, cf_scr, *, tiles_per_seq):
    ts, c = o_ref.shape
    first = (pl.program_id(0) % tiles_per_seq) == 0
    glu = v_ref[...] * _sigmoid(g_ref[...])
    _fill_history(scr, first, vh_ref[...] * _sigmoid(gh_ref[...]), glu, CF_HALO)
    n_sh = sh_scr.shape[1]
    for r in range(1, SUBLANE):
        sh_scr[r - 1] = scr[SUBLANE - r:SUBLANE - r + n_sh, :]
    kw = w_ref.shape[0]
    s1 = jnp.zeros((ts, LANE), F32)
    for c0 in range(0, c, LANE):
        acc = jnp.zeros((ts, LANE), F32) + cb_ref[:, c0:c0 + LANE]
        for d in range(kw):
            a, r = divmod(d, SUBLANE)
            if r == 0:
                tap = scr[CF_HALO - d:CF_HALO - d + ts, c0:c0 + LANE]
            else:
                off = CF_HALO - SUBLANE * (a + 1)
                tap = sh_scr[r - 1, off:off + ts, c0:c0 + LANE]
            acc = acc + w_ref[kw - 1 - d:kw - d, c0:c0 + LANE] * tap
        cf_scr[:, c0:c0 + LANE] = acc
        s1 = s1 + acc
    mu = jnp.sum(s1, axis=-1, keepdims=True) * (1.0 / c)
    s2 = jnp.zeros((ts, LANE), F32)
    for c0 in range(0, c, LANE):
        xc = cf_scr[:, c0:c0 + LANE] - mu
        s2 = s2 + xc * xc
    rstd = lax.rsqrt(jnp.sum(s2, axis=-1, keepdims=True) * (1.0 / c) + NORM_EPS)
    for c0 in range(0, c, LANE):
        y = (cf_scr[:, c0:c0 + LANE] - mu) * rstd * lw_ref[:, c0:c0 + LANE] + lb_ref[:, c0:c0 + LANE]
        o_ref[:, c0:c0 + LANE] = _silu(y).astype(o_ref.dtype)


def mix_conformer(sv, sg, w, cb, lw, lb, seq, ts=128):
    m = sv[0].shape[0]
    c = w.shape[1]
    kern = functools.partial(_mix_conformer_kernel, tiles_per_seq=seq // ts)
    vec = pl.BlockSpec((1, c), lambda i: (0, 0))
    return pl.pallas_call(
        kern,
        grid=(m // ts,),
        in_specs=[_main_spec(ts, c, sv[1]), _main_spec(ts, c, sg[1]),
                  _halo_spec(ts, CF_HALO, c, sv[1]), _halo_spec(ts, CF_HALO, c, sg[1]),
                  pl.BlockSpec(w.shape, lambda i: (0, 0)), vec, vec, vec],
        out_specs=pl.BlockSpec((ts, c), lambda i: (i, 0)),
        out_shape=jax.ShapeDtypeStruct((m, c), BF16),
        scratch_shapes=[pltpu.VMEM((CF_HALO + ts, c), F32),
                        pltpu.VMEM((SUBLANE - 1, CF_HALO - SUBLANE + ts, c), F32),
                        pltpu.VMEM((ts, c), F32)],
        compiler_params=_cparams(("parallel",)),
        name="mix_conformer",
    )(sv[0], sg[0], sv[0], sg[0], w, cb.reshape(1, c), lw.reshape(1, c), lb.reshape(1, c))


POOL_HALO = 16


def _mix_pool_kernel(u_ref, uh_ref, pw_ref, ps_ref, o_ref, scr, *, tiles_per_seq):
    ts, c = o_ref.shape
    ng = pw_ref.shape[0]
    cg = c // ng
    tile = pl.program_id(0) % tiles_per_seq
    first = tile == 0
    _fill_history(scr, first, uh_ref[...], u_ref[...], POOL_HALO)
    pos = tile * ts + lax.broadcasted_iota(jnp.int32, (ts, 1), 0)
    for gi, win in enumerate(POOL_WINDOWS):
        c0 = gi * cg
        cur = scr[POOL_HALO:POOL_HALO + ts, c0:c0 + cg]
        acc = cur
        for d in range(1, win):
            acc = acc + scr[POOL_HALO - d:POOL_HALO - d + ts, c0:c0 + cg]
        cnt = jnp.minimum(pos + 1, win).astype(F32)
        p = acc / cnt - cur
        y = _dot_bf16(p, pw_ref[gi])
        o_ref[:, c0:c0 + cg] = (y * ps_ref[:, c0:c0 + cg]).astype(o_ref.dtype)


def mix_pool(su, pool_w, pool_scale, seq, ts=256):
    m = su[0].shape[0]
    ng, cg, _ = pool_w.shape
    c = ng * cg
    kern = functools.partial(_mix_pool_kernel, tiles_per_seq=seq // ts)
    return pl.pallas_call(
        kern,
        grid=(m // ts,),
        in_specs=[_main_spec(ts, c, su[1]), _halo_spec(ts, POOL_HALO, c, su[1]),
                  pl.BlockSpec(pool_w.shape, lambda i: (0, 0, 0)),
                  pl.BlockSpec((1, c), lambda i: (0, 0))],
        out_specs=pl.BlockSpec((ts, c), lambda i: (i, 0)),
        out_shape=jax.ShapeDtypeStruct((m, c), BF16),
        scratch_shapes=[pltpu.VMEM((POOL_HALO + ts, c), F32)],
        compiler_params=_cparams(("parallel",)),
        name="mix_pool",
    )(su[0], su[0], pool_w.astype(BF16), pool_scale.reshape(1, c))


def _split3(x):
    x1 = x.astype(BF16)
    r1 = x - x1.astype(F32)
    x2 = r1.astype(BF16)
    x3 = (r1 - x2.astype(F32)).astype(BF16)
    return x1, x2, x3


P1_GROUP = 16


def _gdn_kernel(q_ref, k_ref, v_ref, qh_ref, kh_ref, vh_ref, z_ref, ab_ref,
                wq_ref, wk_ref, wv_ref, alog_ref, dtb_ref, nw_ref, o_ref,
                st_ref, hist, qs, ks, vs, gc_s, gcr_s, kn_s, kbq_s, vk_s, wqd_s, kd_s, u_s, a_s,
                *, n_heads):
    ts = o_ref.shape[0]
    dh = GDN_HEAD_DIM
    cc = GDN_CHUNK
    n_chunks = ts // cc
    first = pl.program_id(1) == 0

    @pl.when(first)
    def _():
        st_ref[...] = jnp.zeros_like(st_ref)

    def conv_silu(x_ref, xh_ref, w_ref, dst):
        _fill_history(hist, first, xh_ref[...], x_ref[...], SUBLANE)
        kw = w_ref.shape[0]
        acc = w_ref[kw - 1:kw, :] * x_ref[...]
        for d in range(1, kw):
            acc = acc + w_ref[kw - 1 - d:kw - d, :] * hist[SUBLANE - d:SUBLANE - d + ts, :]
        dst[...] = _silu(acc)

    conv_silu(q_ref, qh_ref, wq_ref, qs)
    conv_silu(k_ref, kh_ref, wk_ref, ks)
    conv_silu(v_ref, vh_ref, wv_ref, vs)

    ab = ab_ref[...]
    xg = ab + dtb_ref[...]
    softplus = jnp.maximum(xg, 0.0) + jnp.log1p(jnp.exp(-jnp.abs(xg)))
    g_all = -jnp.exp(alog_ref[...]) * softplus
    beta_all = _sigmoid(ab)

    rt = lax.broadcasted_iota(jnp.int32, (ts, ts), 0)
    ct = lax.broadcasted_iota(jnp.int32, (ts, ts), 1)
    tri_blk = jnp.where((rt >= ct) & ((rt // cc) == (ct // cc)), 1.0, 0.0).astype(BF16)
    g1, g2, g3 = _split3(g_all)
    gc_all = (jnp.dot(tri_blk, g1, preferred_element_type=F32)
              + jnp.dot(tri_blk, g2, preferred_element_type=F32)
              + jnp.dot(tri_blk, g3, preferred_element_type=F32))
    gc_s[...] = gc_all
    gcr_s[...] = gc_all.T

    scale = dh ** -0.5

    for h in range(n_heads):
        lanes = slice(h * dh, (h + 1) * dh)
        gc_col = jnp.broadcast_to(gc_all[:, h:h + 1], (ts, dh))
        b_col = jnp.broadcast_to(beta_all[:, n_heads + h:n_heads + h + 1], (ts, dh))
        q_all = qs[:, lanes]
        k_all = ks[:, lanes]
        q_all = q_all * (lax.rsqrt(jnp.sum(q_all * q_all, axis=-1, keepdims=True) + NORM_EPS) * scale)
        k_all = k_all * lax.rsqrt(jnp.sum(k_all * k_all, axis=-1, keepdims=True) + NORM_EPS)
        egc_all = jnp.exp(gc_col)
        kb_all = k_all * b_col
        kn_s[:, lanes] = k_all.astype(BF16)
        vk_s[:, 2 * h * dh:(2 * h + 1) * dh] = (vs[:, lanes] * b_col).astype(BF16)
        vk_s[:, (2 * h + 1) * dh:(2 * h + 2) * dh] = (kb_all * egc_all).astype(BF16)
        qd_all = q_all * egc_all
        for c in range(n_chunks):
            rows = slice(c * cc, (c + 1) * cc)
            gc_last = gc_col[(c + 1) * cc - 1:(c + 1) * cc, :]
            kbq_s[2 * c * cc:(2 * c + 1) * cc, lanes] = kb_all[rows].astype(BF16)
            kbq_s[(2 * c + 1) * cc:(2 * c + 2) * cc, lanes] = q_all[rows].astype(BF16)
            wqd_s[(2 * c + 1) * cc:(2 * c + 2) * cc, lanes] = qd_all[rows].astype(BF16)
            kd_s[rows, lanes] = (k_all[rows] * jnp.exp(gc_last - gc_col[rows])).astype(BF16)

    row = lax.broadcasted_iota(jnp.int32, (cc, cc), 0)
    col = lax.broadcasted_iota(jnp.int32, (cc, cc), 1)
    eye = (row == col).astype(F32)

    def p1_stages(c, h):
        rows = slice(c * cc, (c + 1) * cc)
        lanes = slice(h * dh, (h + 1) * dh)
        s = {}

        def s0():
            s["kq"] = lax.dot_general(kbq_s[2 * c * cc:(2 * c + 2) * cc, lanes], kn_s[rows, lanes], _NT,
                                      preferred_element_type=F32)

        def s1():
            pdiff = jnp.broadcast_to(gc_s[rows, h:h + 1], (cc, cc)) - gcr_s[h:h + 1, rows]
            decay = jnp.exp(jnp.where(row >= col, pdiff, -jnp.inf))
            kq = s.pop("kq")
            s["m"] = jnp.where(row > col, kq[:cc] * decay, 0.0)
            a_s[rows, h * cc:(h + 1) * cc] = (kq[cc:] * decay).astype(BF16)
            s["m1"] = s["m"].astype(BF16)
            s["p2"] = _dot_bf16(s["m1"], s["m1"])

        def s2():
            s["p2b"] = s["p2"].astype(BF16)
            s["p4"] = _dot_bf16(s["p2b"], s["p2b"])
            s["mp2"] = _dot_bf16(s.pop("m1"), s["p2b"])

        def s3():
            s["p4b"] = s["p4"].astype(BF16)
            s["p8"] = _dot_bf16(s["p4b"], s["p4b"])
            s["x1"] = eye - s.pop("m") + s.pop("p2") - s.pop("mp2")
            s.pop("p2b")

        def s4():
            p8b = s["p8"].astype(BF16)
            s["p16"] = _dot_bf16(p8b, p8b)
            s["p12"] = _dot_bf16(s.pop("p4b"), p8b)

        def s5():
            s["p16b"] = s["p16"].astype(BF16)
            s["p32"] = _dot_bf16(s["p16b"], s["p16b"])
            x2 = eye + s.pop("p4") + s.pop("p8") + s.pop("p12")
            s["x12"] = _dot_bf16(s.pop("x1"), x2)

        def s6():
            s["p48"] = _dot_bf16(s.pop("p16b"), s["p32"])

        def s7():
            x3 = eye + s.pop("p16") + s.pop("p32") + s.pop("p48")
            s["t"] = _dot_bf16(s.pop("x12"), x3)

        def s8():
            uw = _dot_bf16(s.pop("t"), vk_s[rows, 2 * h * dh:(2 * h + 2) * dh])
            u_s[rows, lanes] = uw[:, :dh]
            wqd_s[2 * c * cc:(2 * c + 1) * cc, lanes] = uw[:, dh:].astype(BF16)

        return (s0, s1, s2, s3, s4, s5, s6, s7, s8)

    chains = [(c, h) for c in range(n_chunks) for h in range(n_heads)]
    for g0 in range(0, len(chains), P1_GROUP):
        group = [p1_stages(c, h) for c, h in chains[g0:g0 + P1_GROUP]]
        for stage in zip(*group):
            for fn in stage:
                fn()

    for c in range(n_chunks):
        rows = slice(c * cc, (c + 1) * cc)
        states = [st_ref[h] for h in range(n_heads)]
        ws_qs = [lax.dot_general(wqd_s[2 * c * cc:(2 * c + 2) * cc, h * dh:(h + 1) * dh],
                                 states[h].astype(BF16), _NN, preferred_element_type=F32)
                 for h in range(n_heads)]
        v_new = [(u_s[rows, h * dh:(h + 1) * dh] - ws_qs[h][:cc]).astype(BF16) for h in range(n_heads)]
        o_intra = [lax.dot_general(a_s[rows, h * cc:(h + 1) * cc], v_new[h], _NN, preferred_element_type=F32)
                   for h in range(n_heads)]
        s_upd = [lax.dot_general(kd_s[rows, h * dh:(h + 1) * dh], v_new[h], _TN, preferred_element_type=F32)
                 for h in range(n_heads)]
        for h in range(n_heads):
            lanes = slice(h * dh, (h + 1) * dh)
            g_last = jnp.exp(jnp.broadcast_to(gc_s[(c + 1) * cc - 1:(c + 1) * cc, h:h + 1], (1, dh)))
            st_ref[h] = states[h] * g_last + s_upd[h]
            o = ws_qs[h][cc:] + o_intra[h]
            o = o * lax.rsqrt(jnp.mean(o * o, axis=-1, keepdims=True) + NORM_EPS) * nw_ref[...]
            o_ref[rows, lanes] = (o * _silu(z_ref[rows, lanes])).astype(o_ref.dtype)


def mix_gdn(sq, sk, sv, sz, sab, conv_w, a_log, dt_bias, norm_w, batch, seq, ts=256):
    m = sq[0].shape[0]
    c = conv_w.shape[1] // 3
    n_heads = c // GDN_HEAD_DIM
    nt = seq // ts
    r = ts // SUBLANE

    def main(colblk):
        return pl.BlockSpec((ts, c), lambda b, s: (b * nt + s, colblk))

    def halo(colblk):
        return pl.BlockSpec((SUBLANE, c), lambda b, s: (jnp.maximum((b * nt + s) * r - 1, 0), colblk))

    def wspec(j):
        return pl.BlockSpec((conv_w.shape[0], c), lambda b, s: (0, j))

    vec = pl.BlockSpec((1, LANE), lambda b, s: (0, 0))
    pad = LANE - n_heads
    alog = jnp.pad(a_log, (0, pad)).reshape(1, LANE)
    dtb = jnp.pad(dt_bias, (0, pad)).reshape(1, LANE)
    kern = functools.partial(_gdn_kernel, n_heads=n_heads)
    return pl.pallas_call(
        kern,
        grid=(batch, nt),
        in_specs=[main(sq[1]), main(sk[1]), main(sv[1]),
                  halo(sq[1]), halo(sk[1]), halo(sv[1]),
                  main(sz[1]),
                  pl.BlockSpec((ts, LANE), lambda b, s: (b * nt + s, sab[1])),
                  wspec(0), wspec(1), wspec(2), vec, vec, vec],
        out_specs=pl.BlockSpec((ts, c), lambda b, s: (b * nt + s, 0)),
        out_shape=jax.ShapeDtypeStruct((m, c), BF16),
        scratch_shapes=[pltpu.VMEM((n_heads, GDN_HEAD_DIM, GDN_HEAD_DIM), F32),
                        pltpu.VMEM((SUBLANE + ts, c), F32),
                        pltpu.VMEM((ts, c), F32),
                        pltpu.VMEM((ts, c), F32),
                        pltpu.VMEM((ts, c), F32),
                        pltpu.VMEM((ts, LANE), F32),
                        pltpu.VMEM((LANE, ts), F32),
                        pltpu.VMEM((ts, c), BF16),
                        pltpu.VMEM((2 * ts, c), BF16),
                        pltpu.VMEM((ts, 2 * c), BF16),
                        pltpu.VMEM((2 * ts, c), BF16),
                        pltpu.VMEM((ts, c), BF16),
                        pltpu.VMEM((ts, c), F32),
                        pltpu.VMEM((ts, n_heads * GDN_CHUNK), BF16)],
        compiler_params=_cparams(("arbitrary", "arbitrary")),
        name="mix_gdn",
    )(sq[0], sk[0], sv[0], sq[0], sk[0], sv[0], sz[0], sab[0],
      conv_w, conv_w, conv_w, alog, dtb, norm_w.reshape(1, LANE))


FF_TILE = 256


def _layer(x2, batch, seq, li, p, stacked):
    cg = p["sc_conv_w"].shape[1]
    n_heads = cg // GDN_HEAD_DIM
    n_main = 9 * cg
    n_ab = 2 * n_heads

    h = rmsnorm(x2, p["attn_norm_w"], BF16)
    proj = matmul_in_proj(h, stacked["w_in_t"], li, n_main, n_ab, cg, tm=2048)

    y_a = mix_sconv((proj, 0), (proj, 1), (proj, 2), p["sc_conv_w"], seq)
    y_b = mix_conformer((proj, 3), (proj, 4), p["cf_conv_w"], p["cf_conv_b"], p["cf_ln_w"], p["cf_ln_b"], seq)
    y_c = mix_gdn((proj, 5), (proj, 6), (proj, 7), (proj, 8), (proj, (n_main + cg) // LANE),
                  p["gdn_conv_w"], p["gdn_a_log"], p["gdn_dt_bias"], p["gdn_norm_w"], batch, seq)
    y_d = mix_pool((proj, 9), p["pool_w"], p["pool_scale"], seq)

    x2 = matmul4_residual((y_a, y_b, y_c, y_d), stacked["w_out"], li, x2, tm=1024, tn=512)

    wd = cast_weight(stacked["w_down"], li)
    h = rmsnorm(x2, p["ffn_norm_w"], BF16)
    act = matmul_swiglu(h, stacked["w_gate"], stacked["w_up"], li, tm=1024, tn=FF_TILE)
    x2 = matmul_residual(act, wd, x2, tm=512, tn=512)
    return x2


def kernel(x, attn_norm_w, w_in, sc_conv_w, cf_conv_w, cf_conv_b, cf_ln_w, cf_ln_b, gdn_conv_w,
           gdn_a_log, gdn_dt_bias, gdn_norm_w, pool_w, pool_scale, w_out, ffn_norm_w, w_gate, w_up,
           w_down, final_norm_w):
    batch, seq, d = x.shape
    small = dict(attn_norm_w=attn_norm_w, sc_conv_w=sc_conv_w, cf_conv_w=cf_conv_w,
                 cf_conv_b=cf_conv_b, cf_ln_w=cf_ln_w, cf_ln_b=cf_ln_b, gdn_conv_w=gdn_conv_w,
                 gdn_a_log=gdn_a_log, gdn_dt_bias=gdn_dt_bias, gdn_norm_w=gdn_norm_w, pool_w=pool_w,
                 pool_scale=pool_scale, ffn_norm_w=ffn_norm_w)
    stacked = dict(w_in_t=jnp.swapaxes(w_in, 1, 2), w_out=w_out, w_gate=w_gate, w_up=w_up, w_down=w_down)
    x2 = x.reshape(batch * seq, d)
    for li in range(attn_norm_w.shape[0]):
        x2 = _layer(x2, batch, seq, li, {name: val[li] for name, val in small.items()}, stacked)
    out = rmsnorm(x2, final_norm_w, x.dtype)
    return out.reshape(batch, seq, d)
```

```python
import functools

import jax
import jax.numpy as jnp
from jax import lax
from jax.experimental import pallas as pl
from jax.experimental.pallas import tpu as pltpu

NORM_EPS = 1e-6
GDN_HEAD_DIM = 128
GDN_CHUNK = 64
POOL_WINDOWS = (2, 4, 8, 16)
LANE = 128
SUBLANE = 8
VMEM_LIMIT = 56 * 1024 * 1024

BF16 = jnp.bfloat16
F32 = jnp.float32


def _cparams(sem):
    return pltpu.CompilerParams(dimension_semantics=sem, vmem_limit_bytes=VMEM_LIMIT)


def _sigmoid(x):
    return 1.0 / (1.0 + jnp.exp(-x))


def _silu(x):
    return x * _sigmoid(x)


_NN = (((1,), (0,)), ((), ()))
_NT = (((1,), (1,)), ((), ()))
_TN = (((0,), (0,)), ((), ()))


def _dot_bf16(a, b, dims=_NN):
    return lax.dot_general(a.astype(BF16), b.astype(BF16), dims, preferred_element_type=F32)


def _cast_kernel(x_ref, o_ref):
    o_ref[...] = x_ref[...].astype(o_ref.dtype)


def cast_weight(w_stacked, layer, tk=128):
    _, k, n = w_stacked.shape
    return pl.pallas_call(
        _cast_kernel,
        grid=(k // tk,),
        in_specs=[pl.BlockSpec((None, tk, n), lambda r: (layer, r, 0))],
        out_specs=pl.BlockSpec((tk, n), lambda r: (r, 0)),
        out_shape=jax.ShapeDtypeStruct((k, n), BF16),
        compiler_params=_cparams(("parallel",)),
        name="cast_weight",
    )(w_stacked)


IN_TILE = 512


def _rmsnorm_kernel(x_ref, w_ref, o_ref):
    x = x_ref[...]
    ms = jnp.mean(x * x, axis=-1, keepdims=True)
    o_ref[...] = (x * lax.rsqrt(ms + NORM_EPS) * w_ref[...]).astype(o_ref.dtype)


def rmsnorm(x, w, out_dtype, tm=256):
    m, d = x.shape
    return pl.pallas_call(
        _rmsnorm_kernel,
        grid=(m // tm,),
        in_specs=[pl.BlockSpec((tm, d), lambda i: (i, 0)),
                  pl.BlockSpec((1, d), lambda i: (0, 0))],
        out_specs=pl.BlockSpec((tm, d), lambda i: (i, 0)),
        out_shape=jax.ShapeDtypeStruct((m, d), out_dtype),
        compiler_params=_cparams(("parallel",)),
        name="rmsnorm",
    )(x, w.reshape(1, d))


def _panel_spec(tm, k):
    if tm > 1024:
        return pl.BlockSpec((tm, k), lambda i, j: (i, 0), pipeline_mode=pl.Buffered(1))
    return pl.BlockSpec((tm, k), lambda i, j: (i, 0))


def _mm_in_kernel(a_ref, bt_ref, o_ref):
    o_ref[...] = lax.dot_general(a_ref[...], bt_ref[...].astype(BF16), _NT, preferred_element_type=F32)


def matmul_in_proj(a, w_in_t, layer, n_main, n_ab, n_pool, tm):
    m, k = a.shape
    _, n, _ = w_in_t.shape
    main_blocks = n_main // IN_TILE
    ab_block = main_blocks + n_pool // IN_TILE

    def src_row(j):
        return jnp.where(j < main_blocks, j * IN_TILE,
                         jnp.where(j < ab_block, n_main + n_ab + (j - main_blocks) * IN_TILE, n_main))

    assert n_main + IN_TILE <= n
    return pl.pallas_call(
        _mm_in_kernel,
        grid=(m // tm, ab_block + 1),
        in_specs=[_panel_spec(tm, k),
                  pl.BlockSpec((pl.Element(IN_TILE), pl.Element(k)),
                               lambda i, j: (pl.multiple_of(layer * n + src_row(j), SUBLANE), 0))],
        out_specs=pl.BlockSpec((tm, IN_TILE), lambda i, j: (i, j)),
        out_shape=jax.ShapeDtypeStruct((m, (ab_block + 1) * IN_TILE), F32),
        compiler_params=_cparams(("parallel", "arbitrary")),
        name="in_proj",
    )(a, w_in_t.reshape(-1, k))


def _mm_res_kernel(a_ref, b_ref, r_ref, o_ref):
    o_ref[...] = r_ref[...] + jnp.dot(a_ref[...], b_ref[...], preferred_element_type=F32)


def matmul_residual(a, b, res, tm, tn):
    m, k = a.shape
    n = b.shape[1]
    return pl.pallas_call(
        _mm_res_kernel,
        grid=(m // tm, n // tn),
        in_specs=[pl.BlockSpec((tm, k), lambda i, j: (i, 0)),
                  pl.BlockSpec((k, tn), lambda i, j: (0, j)),
                  pl.BlockSpec((tm, tn), lambda i, j: (i, j))],
        out_specs=pl.BlockSpec((tm, tn), lambda i, j: (i, j)),
        out_shape=jax.ShapeDtypeStruct((m, n), F32),
        compiler_params=_cparams(("parallel", "arbitrary")),
        name="matmul_residual",
    )(a, b, res)


def _layer_tile_spec(k, tn, layer):
    return pl.BlockSpec((None, k, tn), lambda i, j: (layer, 0, j))


def _mm4_res_kernel(a0, a1, a2, a3, b_ref, r_ref, o_ref):
    kg = a0.shape[1]
    acc = r_ref[...]
    for g, a in enumerate((a0, a1, a2, a3)):
        acc = acc + jnp.dot(a[...], b_ref[g * kg:(g + 1) * kg, :].astype(BF16), preferred_element_type=F32)
    o_ref[...] = acc


def matmul4_residual(parts, w_stacked, layer, res, tm, tn):
    m, kg = parts[0].shape
    _, k, n = w_stacked.shape
    a_spec = pl.BlockSpec((tm, kg), lambda i, j: (i, 0))
    return pl.pallas_call(
        _mm4_res_kernel,
        grid=(m // tm, n // tn),
        in_specs=[a_spec, a_spec, a_spec, a_spec,
                  _layer_tile_spec(k, tn, layer),
                  pl.BlockSpec((tm, tn), lambda i, j: (i, j))],
        out_specs=pl.BlockSpec((tm, tn), lambda i, j: (i, j)),
        out_shape=jax.ShapeDtypeStruct((m, n), F32),
        compiler_params=_cparams(("parallel", "arbitrary")),
        name="out_proj",
    )(*parts, w_stacked, res)


def _mm_glu_kernel(a_ref, bg_ref, bu_ref, o_ref, *, last_width):
    tn = o_ref.shape[1]

    def tile(width):
        a = a_ref[...]
        g = jnp.dot(a, bg_ref[:, :width].astype(BF16), preferred_element_type=F32)
        u = jnp.dot(a, bu_ref[:, :width].astype(BF16), preferred_element_type=F32)
        o_ref[:, :width] = (_silu(g) * u).astype(o_ref.dtype)

    if last_width == tn:
        tile(tn)
    else:
        is_last = pl.program_id(1) == pl.num_programs(1) - 1
        pl.when(jnp.logical_not(is_last))(lambda: tile(tn))
        pl.when(is_last)(lambda: tile(last_width))


def matmul_swiglu(a, wg_stacked, wu_stacked, layer, tm, tn):
    m, k = a.shape
    n = wg_stacked.shape[2]
    b_spec = _layer_tile_spec(k, tn, layer)
    last_width = n - (pl.cdiv(n, tn) - 1) * tn
    return pl.pallas_call(
        functools.partial(_mm_glu_kernel, last_width=last_width),
        grid=(m // tm, pl.cdiv(n, tn)),
        in_specs=[pl.BlockSpec((tm, k), lambda i, j: (i, 0), pipeline_mode=pl.Buffered(1)), b_spec, b_spec],
        out_specs=pl.BlockSpec((tm, tn), lambda i, j: (i, j)),
        out_shape=jax.ShapeDtypeStruct((m, n), BF16),
        compiler_params=_cparams(("parallel", "arbitrary")),
        name="ffn_gate_up",
    )(a, wg_stacked, wu_stacked)


def _halo_spec(ts, hb, c, col):
    r = ts // hb
    return pl.BlockSpec((hb, c), lambda i: (jnp.maximum(i * r - 1, 0), col))


def _main_spec(ts, c, col):
    return pl.BlockSpec((ts, c), lambda i: (i, col))


def _fill_history(scr, first, halo, cur, hb):
    scr[0:hb, :] = jnp.where(first, 0.0, halo)
    scr[hb:, :] = cur


def _mix_sconv_kernel(b_ref, c_ref, h_ref, ch_ref, hh_ref, w_ref, o_ref, scr, *, tiles_per_seq):
    ts = o_ref.shape[0]
    first = (pl.program_id(0) % tiles_per_seq) == 0
    m = c_ref[...] * h_ref[...]
    _fill_history(scr, first, ch_ref[...] * hh_ref[...], m, SUBLANE)
    w = w_ref[...]
    kw = w.shape[0]
    y = w[kw - 1:kw, :] * m
    for d in range(1, kw):
        y = y + w[kw - 1 - d:kw - d, :] * scr[SUBLANE - d:SUBLANE - d + ts, :]
    o_ref[...] = (b_ref[...] * y).astype(o_ref.dtype)


def mix_sconv(sb, sc, sh, w, seq, ts=256):
    m = sb[0].shape[0]
    c = w.shape[1]
    kern = functools.partial(_mix_sconv_kernel, tiles_per_seq=seq // ts)
    return pl.pallas_call(
        kern,
        grid=(m // ts,),
        in_specs=[_main_spec(ts, c, sb[1]), _main_spec(ts, c, sc[1]), _main_spec(ts, c, sh[1]),
                  _halo_spec(ts, SUBLANE, c, sc[1]), _halo_spec(ts, SUBLANE, c, sh[1]),
                  pl.BlockSpec(w.shape, lambda i: (0, 0))],
        out_specs=pl.BlockSpec((ts, c), lambda i: (i, 0)),
        out_shape=jax.ShapeDtypeStruct((m, c), BF16),
        scratch_shapes=[pltpu.VMEM((SUBLANE + ts, c), F32)],
        compiler_params=_cparams(("parallel",)),
        name="mix_sconv",
    )(sb[0], sc[0], sh[0], sc[0], sh[0], w)


CF_HALO = 32


def _mix_conformer_kernel(v_ref, g_ref, vh_ref, gh_ref, w_ref, cb_ref, lw_ref, lb_ref, o_ref,
                          scr, sh_scr, cf_scr, *, tiles_per_seq):
    ts, c = o_ref.shape
    first = (pl.program_id(0) % tiles_per_seq) == 0
    glu = v_ref[...] * _sigmoid(g_ref[...])
    _fill_history(scr, first, vh_ref[...] * _sigmoid(gh_ref[...]), glu, CF_HALO)
    n_sh = sh_scr.shape[1]
    for r in range(1, SUBLANE):
        sh_scr[r - 1] = scr[SUBLANE - r:SUBLANE - r + n_sh, :]
    kw = w_ref.shape[0]
    s1 = jnp.zeros((ts, LANE), F32)
    for c0 in range(0, c, LANE):
        acc = jnp.zeros((ts, LANE), F32) + cb_ref[:, c0:c0 + LANE]
        for d in range(kw):
            a, r = divmod(d, SUBLANE)
            if r == 0:
                tap = scr[CF_HALO - d:CF_HALO - d + ts, c0:c0 + LANE]
            else:
                off = CF_HALO - SUBLANE * (a + 1)
                tap = sh_scr[r - 1, off:off + ts, c0:c0 + LANE]
            acc = acc + w_ref[kw - 1 - d:kw - d, c0:c0 + LANE] * tap
        cf_scr[:, c0:c0 + LANE] = acc
        s1 = s1 + acc
    mu = jnp.sum(s1, axis=-1, keepdims=True) * (1.0 / c)
    s2 = jnp.zeros((ts, LANE), F32)
    for c0 in range(0, c, LANE):
        xc = cf_scr[:, c0:c0 + LANE] - mu
        s2 = s2 + xc * xc
    rstd = lax.rsqrt(jnp.sum(s2, axis=-1, keepdims=True) * (1.0 / c) + NORM_EPS)
    for c0 in range(0, c, LANE):
        y = (cf_scr[:, c0:c0 + LANE] - mu) * rstd * lw_ref[:, c0:c0 + LANE] + lb_ref[:, c0:c0 + LANE]
        o_ref[:, c0:c0 + LANE] = _silu(y).astype(o_ref.dtype)


def mix_conformer(sv, sg, w, cb, lw, lb, seq, ts=128):
    m = sv[0].shape[0]
    c = w.shape[1]
    kern = functools.partial(_mix_conformer_kernel, tiles_per_seq=seq // ts)
    vec = pl.BlockSpec((1, c), lambda i: (0, 0))
    return pl.pallas_call(
        kern,
        grid=(m // ts,),
        in_specs=[_main_spec(ts, c, sv[1]), _main_spec(ts, c, sg[1]),
                  _halo_spec(ts, CF_HALO, c, sv[1]), _halo_spec(ts, CF_HALO, c, sg[1]),
                  pl.BlockSpec(w.shape, lambda i: (0, 0)), vec, vec, vec],
        out_specs=pl.BlockSpec((ts, c), lambda i: (i, 0)),
        out_shape=jax.ShapeDtypeStruct((m, c), BF16),
        scratch_shapes=[pltpu.VMEM((CF_HALO + ts, c), F32),
                        pltpu.VMEM((SUBLANE - 1, CF_HALO - SUBLANE + ts, c), F32),
                        pltpu.VMEM((ts, c), F32)],
        compiler_params=_cparams(("parallel",)),
        name="mix_conformer",
    )(sv[0], sg[0], sv[0], sg[0], w, cb.reshape(1, c), lw.reshape(1, c), lb.reshape(1, c))


POOL_HALO = 16


def _mix_pool_kernel(u_ref, uh_ref, pw_ref, ps_ref, o_ref, scr, *, tiles_per_seq):
    ts, c = o_ref.shape
    ng = pw_ref.shape[0]
    cg = c // ng
    tile = pl.program_id(0) % tiles_per_seq
    first = tile == 0
    _fill_history(scr, first, uh_ref[...], u_ref[...], POOL_HALO)
    pos = tile * ts + lax.broadcasted_iota(jnp.int32, (ts, 1), 0)
    for gi, win in enumerate(POOL_WINDOWS):
        c0 = gi * cg
        cur = scr[POOL_HALO:POOL_HALO + ts, c0:c0 + cg]
        acc = cur
        for d in range(1, win):
            acc = acc + scr[POOL_HALO - d:POOL_HALO - d + ts, c0:c0 + cg]
        cnt = jnp.minimum(pos + 1, win).astype(F32)
        p = acc / cnt - cur
        y = _dot_bf16(p, pw_ref[gi])
        o_ref[:, c0:c0 + cg] = (y * ps_ref[:, c0:c0 + cg]).astype(o_ref.dtype)


def mix_pool(su, pool_w, pool_scale, seq, ts=256):
    m = su[0].shape[0]
    ng, cg, _ = pool_w.shape
    c = ng * cg
    kern = functools.partial(_mix_pool_kernel, tiles_per_seq=seq // ts)
    return pl.pallas_call(
        kern,
        grid=(m // ts,),
        in_specs=[_main_spec(ts, c, su[1]), _halo_spec(ts, POOL_HALO, c, su[1]),
                  pl.BlockSpec(pool_w.shape, lambda i: (0, 0, 0)),
                  pl.BlockSpec((1, c), lambda i: (0, 0))],
        out_specs=pl.BlockSpec((ts, c), lambda i: (i, 0)),
        out_shape=jax.ShapeDtypeStruct((m, c), BF16),
        scratch_shapes=[pltpu.VMEM((POOL_HALO + ts, c), F32)],
        compiler_params=_cparams(("parallel",)),
        name="mix_pool",
    )(su[0], su[0], pool_w.astype(BF16), pool_scale.reshape(1, c))


def _split3(x):
    x1 = x.astype(BF16)
    r1 = x - x1.astype(F32)
    x2 = r1.astype(BF16)
    x3 = (r1 - x2.astype(F32)).astype(BF16)
    return x1, x2, x3


P1_GROUP = 16


def _gdn_kernel(q_ref, k_ref, v_ref, qh_ref, kh_ref, vh_ref, z_ref, ab_ref,
                wq_ref, wk_ref, wv_ref, alog_ref, dtb_ref, nw_ref, o_ref,
                st_ref, hist, qs, ks, vs, gc_s, gcr_s, kn_s, kbq_s, vk_s, wqd_s, kd_s, u_s, a_s,
                *, n_heads):
    ts = o_ref.shape[0]
    dh = GDN_HEAD_DIM
    cc = GDN_CHUNK
    n_chunks = ts // cc
    first = pl.program_id(1) == 0

    @pl.when(first)
    def _():
        st_ref[...] = jnp.zeros_like(st_ref)

    def conv_silu(x_ref, xh_ref, w_ref, dst):
        _fill_history(hist, first, xh_ref[...], x_ref[...], SUBLANE)
        kw = w_ref.shape[0]
        acc = w_ref[kw - 1:kw, :] * x_ref[...]
        for d in range(1, kw):
            acc = acc + w_ref[kw - 1 - d:kw - d, :] * hist[SUBLANE - d:SUBLANE - d + ts, :]
        dst[...] = _silu(acc)

    conv_silu(q_ref, qh_ref, wq_ref, qs)
    conv_silu(k_ref, kh_ref, wk_ref, ks)
    conv_silu(v_ref, vh_ref, wv_ref, vs)

    ab = ab_ref[...]
    xg = ab + dtb_ref[...]
    softplus = jnp.maximum(xg, 0.0) + jnp.log1p(jnp.exp(-jnp.abs(xg)))
    g_all = -jnp.exp(alog_ref[...]) * softplus
    beta_all = _sigmoid(ab)

    rt = lax.broadcasted_iota(jnp.int32, (ts, ts), 0)
    ct = lax.broadcasted_iota(jnp.int32, (ts, ts), 1)
    tri_blk = jnp.where((rt >= ct) & ((rt // cc) == (ct // cc)), 1.0, 0.0).astype(BF16)
    g1, g2, g3 = _split3(g_all)
    gc_all = (jnp.dot(tri_blk, g1, preferred_element_type=F32)
              + jnp.dot(tri_blk, g2, preferred_element_type=F32)
              + jnp.dot(tri_blk, g3, preferred_element_type=F32))
    gc_s[...] = gc_all
    gcr_s[...] = gc_all.T

    scale = dh ** -0.5

    for h in range(n_heads):
        lanes = slice(h * dh, (h + 1) * dh)
        gc_col = jnp.broadcast_to(gc_all[:, h:h + 1], (ts, dh))
        b_col = jnp.broadcast_to(beta_all[:, n_heads + h:n_heads + h + 1], (ts, dh))
        q_all = qs[:, lanes]
        k_all = ks[:, lanes]
        q_all = q_all * (lax.rsqrt(jnp.sum(q_all * q_all, axis=-1, keepdims=True) + NORM_EPS) * scale)
        k_all = k_all * lax.rsqrt(jnp.sum(k_all * k_all, axis=-1, keepdims=True) + NORM_EPS)
        egc_all = jnp.exp(gc_col)
        kb_all = k_all * b_col
        kn_s[:, lanes] = k_all.astype(BF16)
        vk_s[:, 2 * h * dh:(2 * h + 1) * dh] = (vs[:, lanes] * b_col).astype(BF16)
        vk_s[:, (2 * h + 1) * dh:(2 * h + 2) * dh] = (kb_all * egc_all).astype(BF16)
        qd_all = q_all * egc_all
        for c in range(n_chunks):
            rows = slice(c * cc, (c + 1) * cc)
            gc_last = gc_col[(c + 1) * cc - 1:(c + 1) * cc, :]
            kbq_s[2 * c * cc:(2 * c + 1) * cc, lanes] = kb_all[rows].astype(BF16)
            kbq_s[(2 * c + 1) * cc:(2 * c + 2) * cc, lanes] = q_all[rows].astype(BF16)
            wqd_s[(2 * c + 1) * cc:(2 * c + 2) * cc, lanes] = qd_all[rows].astype(BF16)
            kd_s[rows, lanes] = (k_all[rows] * jnp.exp(gc_last - gc_col[rows])).astype(BF16)

    row = lax.broadcasted_iota(jnp.int32, (cc, cc), 0)
    col = lax.broadcasted_iota(jnp.int32, (cc, cc), 1)
    eye = (row == col).astype(F32)

    def p1_stages(c, h):
        rows = slice(c * cc, (c + 1) * cc)
        lanes = slice(h * dh, (h + 1) * dh)
        s = {}

        def s0():
            s["kq"] = lax.dot_general(kbq_s[2 * c * cc:(2 * c + 2) * cc, lanes], kn_s[rows, lanes], _NT,
                                      preferred_element_type=F32)

        def s1():
            pdiff = jnp.broadcast_to(gc_s[rows, h:h + 1], (cc, cc)) - gcr_s[h:h + 1, rows]
            decay = jnp.exp(jnp.where(row >= col, pdiff, -jnp.inf))
            kq = s.pop("kq")
            s["m"] = jnp.where(row > col, kq[:cc] * decay, 0.0)
            a_s[rows, h * cc:(h + 1) * cc] = (kq[cc:] * decay).astype(BF16)
            s["m1"] = s["m"].astype(BF16)
            s["p2"] = _dot_bf16(s["m1"], s["m1"])

        def s2():
            s["p2b"] = s["p2"].astype(BF16)
            s["p4"] = _dot_bf16(s["p2b"], s["p2b"])
            s["mp2"] = _dot_bf16(s.pop("m1"), s["p2b"])

        def s3():
            s["p4b"] = s["p4"].astype(BF16)
            s["p8"] = _dot_bf16(s["p4b"], s["p4b"])
            s["x1"] = eye - s.pop("m") + s.pop("p2") - s.pop("mp2")
            s.pop("p2b")

        def s4():
            p8b = s["p8"].astype(BF16)
            s["p16"] = _dot_bf16(p8b, p8b)
            s["p12"] = _dot_bf16(s.pop("p4b"), p8b)

        def s5():
            s["p16b"] = s["p16"].astype(BF16)
            s["p32"] = _dot_bf16(s["p16b"], s["p16b"])
            x2 = eye + s.pop("p4") + s.pop("p8") + s.pop("p12")
            s["x12"] = _dot_bf16(s.pop("x1"), x2)

        def s6():
            s["p48"] = _dot_bf16(s.pop("p16b"), s["p32"])

        def s7():
            x3 = eye + s.pop("p16") + s.pop("p32") + s.pop("p48")
            s["t"] = _dot_bf16(s.pop("x12"), x3)

        def s8():
            uw = _dot_bf16(s.pop("t"), vk_s[rows, 2 * h * dh:(2 * h + 2) * dh])
            u_s[rows, lanes] = uw[:, :dh]
            wqd_s[2 * c * cc:(2 * c + 1) * cc, lanes] = uw[:, dh:].astype(BF16)

        return (s0, s1, s2, s3, s4, s5, s6, s7, s8)

    chains = [(c, h) for c in range(n_chunks) for h in range(n_heads)]
    for g0 in range(0, len(chains), P1_GROUP):
        group = [p1_stages(c, h) for c, h in chains[g0:g0 + P1_GROUP]]
        for stage in zip(*group):
            for fn in stage:
                fn()

    for c in range(n_chunks):
        rows = slice(c * cc, (c + 1) * cc)
        states = [st_ref[h] for h in range(n_heads)]
        ws_qs = [lax.dot_general(wqd_s[2 * c * cc:(2 * c + 2) * cc, h * dh:(h + 1) * dh],
                                 states[h].astype(BF16), _NN, preferred_element_type=F32)
                 for h in range(n_heads)]
        v_new = [(u_s[rows, h * dh:(h + 1) * dh] - ws_qs[h][:cc]).astype(BF16) for h in range(n_heads)]
        o_intra = [lax.dot_general(a_s[rows, h * cc:(h + 1) * cc], v_new[h], _NN, preferred_element_type=F32)
                   for h in range(n_heads)]
        s_upd = [lax.dot_general(kd_s[rows, h * dh:(h + 1) * dh], v_new[h], _TN, preferred_element_type=F32)
                 for h in range(n_heads)]
        for h in range(n_heads):
            lanes = slice(h * dh, (h + 1) * dh)
            g_last = jnp.exp(jnp.broadcast_to(gc_s[(c + 1) * cc - 1:(c + 1) * cc, h:h + 1], (1, dh)))
            st_ref[h] = states[h] * g_last + s_upd[h]
            o = ws_qs[h][cc:] + o_intra[h]
            o = o * lax.rsqrt(jnp.mean(o * o, axis=-1, keepdims=True) + NORM_EPS) * nw_ref[...]
            o_ref[rows, lanes] = (o * _silu(z_ref[rows, lanes])).astype(o_ref.dtype)


def mix_gdn(sq, sk, sv, sz, sab, conv_w, a_log, dt_bias, norm_w, batch, seq, ts=256):
    m = sq[0].shape[0]
    c = conv_w.shape[1] // 3
    n_heads = c // GDN_HEAD_DIM
    nt = seq // ts
    r = ts // SUBLANE

    def main(colblk):
        return pl.BlockSpec((ts, c), lambda b, s: (b * nt + s, colblk))

    def halo(colblk):
        return pl.BlockSpec((SUBLANE, c), lambda b, s: (jnp.maximum((b * nt + s) * r - 1, 0), colblk))

    def wspec(j):
        return pl.BlockSpec((conv_w.shape[0], c), lambda b, s: (0, j))

    vec = pl.BlockSpec((1, LANE), lambda b, s: (0, 0))
    pad = LANE - n_heads
    alog = jnp.pad(a_log, (0, pad)).reshape(1, LANE)
    dtb = jnp.pad(dt_bias, (0, pad)).reshape(1, LANE)
    kern = functools.partial(_gdn_kernel, n_heads=n_heads)
    return pl.pallas_call(
        kern,
        grid=(batch, nt),
        in_specs=[main(sq[1]), main(sk[1]), main(sv[1]),
                  halo(sq[1]), halo(sk[1]), halo(sv[1]),
                  main(sz[1]),
                  pl.BlockSpec((ts, LANE), lambda b, s: (b * nt + s, sab[1])),
                  wspec(0), wspec(1), wspec(2), vec, vec, vec],
        out_specs=pl.BlockSpec((ts, c), lambda b, s: (b * nt + s, 0)),
        out_shape=jax.ShapeDtypeStruct((m, c), BF16),
        scratch_shapes=[pltpu.VMEM((n_heads, GDN_HEAD_DIM, GDN_HEAD_DIM), F32),
                        pltpu.VMEM((SUBLANE + ts, c), F32),
                        pltpu.VMEM((ts, c), F32),
                        pltpu.VMEM((ts, c), F32),
                        pltpu.VMEM((ts, c), F32),
                        pltpu.VMEM((ts, LANE), F32),
                        pltpu.VMEM((LANE, ts), F32),
                        pltpu.VMEM((ts, c), BF16),
                        pltpu.VMEM((2 * ts, c), BF16),
                        pltpu.VMEM((ts, 2 * c), BF16),
                        pltpu.VMEM((2 * ts, c), BF16),
                        pltpu.VMEM((ts, c), BF16),
                        pltpu.VMEM((ts, c), F32),
                        pltpu.VMEM((ts, n_heads * GDN_CHUNK), BF16)],
        compiler_params=_cparams(("arbitrary", "arbitrary")),
        name="mix_gdn",
    )(sq[0], sk[0], sv[0], sq[0], sk[0], sv[0], sz[0], sab[0],
      conv_w, conv_w, conv_w, alog, dtb, norm_w.reshape(1, LANE))


FF_TILE = 512


def _layer(x2, batch, seq, li, p, stacked):
    cg = p["sc_conv_w"].shape[1]
    n_heads = cg // GDN_HEAD_DIM
    n_main = 9 * cg
    n_ab = 2 * n_heads

    h = rmsnorm(x2, p["attn_norm_w"], BF16)
    proj = matmul_in_proj(h, stacked["w_in_t"], li, n_main, n_ab, cg, tm=2048)

    y_a = mix_sconv((proj, 0), (proj, 1), (proj, 2), p["sc_conv_w"], seq)
    y_b = mix_conformer((proj, 3), (proj, 4), p["cf_conv_w"], p["cf_conv_b"], p["cf_ln_w"], p["cf_ln_b"], seq)
    y_c = mix_gdn((proj, 5), (proj, 6), (proj, 7), (proj, 8), (proj, (n_main + cg) // LANE),
                  p["gdn_conv_w"], p["gdn_a_log"], p["gdn_dt_bias"], p["gdn_norm_w"], batch, seq)
    y_d = mix_pool((proj, 9), p["pool_w"], p["pool_scale"], seq)

    x2 = matmul4_residual((y_a, y_b, y_c, y_d), stacked["w_out"], li, x2, tm=1024, tn=512)

    wd = cast_weight(stacked["w_down"], li)
    h = rmsnorm(x2, p["ffn_norm_w"], BF16)
    act = matmul_swiglu(h, stacked["w_gate"], stacked["w_up"], li, tm=1024, tn=FF_TILE)
    x2 = matmul_residual(act, wd, x2, tm=512, tn=512)
    return x2


def kernel(x, attn_norm_w, w_in, sc_conv_w, cf_conv_w, cf_conv_b, cf_ln_w, cf_ln_b, gdn_conv_w,
           gdn_a_log, gdn_dt_bias, gdn_norm_w, pool_w, pool_scale, w_out, ffn_norm_w, w_gate, w_up,
           w_down, final_norm_w):
    batch, seq, d = x.shape
    small = dict(attn_norm_w=attn_norm_w, sc_conv_w=sc_conv_w, cf_conv_w=cf_conv_w,
                 cf_conv_b=cf_conv_b, cf_ln_w=cf_ln_w, cf_ln_b=cf_ln_b, gdn_conv_w=gdn_conv_w,
                 gdn_a_log=gdn_a_log, gdn_dt_bias=gdn_dt_bias, gdn_norm_w=gdn_norm_w, pool_w=pool_w,
                 pool_scale=pool_scale, ffn_norm_w=ffn_norm_w)
    stacked = dict(w_in_t=jnp.swapaxes(w_in, 1, 2), w_out=w_out, w_gate=w_gate, w_up=w_up, w_down=w_down)
    x2 = x.reshape(batch * seq, d)
    for li in range(attn_norm_w.shape[0]):
        x2 = _layer(x2, batch, seq, li, {name: val[li] for name, val in small.items()}, stacked)
    out = rmsnorm(x2, final_norm_w, x.dtype)
    return out.reshape(batch, seq, d)
```

```python
import functools

import jax
import jax.numpy as jnp
from jax import lax
from jax.experimental import pallas as pl
from jax.experimental.pallas import tpu as pltpu

NORM_EPS = 1e-6
GDN_HEAD_DIM = 128
GDN_CHUNK = 64
POOL_WINDOWS = (2, 4, 8, 16)
LANE = 128
SUBLANE = 8
VMEM_LIMIT = 56 * 1024 * 1024

BF16 = jnp.bfloat16
F32 = jnp.float32


def _cparams(sem):
    return pltpu.CompilerParams(dimension_semantics=sem, vmem_limit_bytes=VMEM_LIMIT)


def _sigmoid(x):
    return 1.0 / (1.0 + jnp.exp(-x))


def _silu(x):
    return x * _sigmoid(x)


_NN = (((1,), (0,)), ((), ()))
_NT = (((1,), (1,)), ((), ()))
_TN = (((0,), (0,)), ((), ()))


def _dot_bf16(a, b, dims=_NN):
    return lax.dot_general(a.astype(BF16), b.astype(BF16), dims, preferred_element_type=F32)


def _cast_kernel(x_ref, o_ref):
    o_ref[...] = x_ref[...].astype(o_ref.dtype)


def cast_weight(w_stacked, layer, tk=128):
    _, k, n = w_stacked.shape
    return pl.pallas_call(
        _cast_kernel,
        grid=(k // tk,),
        in_specs=[pl.BlockSpec((None, tk, n), lambda r: (layer, r, 0))],
        out_specs=pl.BlockSpec((tk, n), lambda r: (r, 0)),
        out_shape=jax.ShapeDtypeStruct((k, n), BF16),
        compiler_params=_cparams(("parallel",)),
        name="cast_weight",
    )(w_stacked)


IN_TILE = 512


def _rmsnorm_kernel(x_ref, w_ref, o_ref):
    x = x_ref[...]
    ms = jnp.mean(x * x, axis=-1, keepdims=True)
    o_ref[...] = (x * lax.rsqrt(ms + NORM_EPS) * w_ref[...]).astype(o_ref.dtype)


def rmsnorm(x, w, out_dtype, tm=256):
    m, d = x.shape
    return pl.pallas_call(
        _rmsnorm_kernel,
        grid=(m // tm,),
        in_specs=[pl.BlockSpec((tm, d), lambda i: (i, 0)),
                  pl.BlockSpec((1, d), lambda i: (0, 0))],
        out_specs=pl.BlockSpec((tm, d), lambda i: (i, 0)),
        out_shape=jax.ShapeDtypeStruct((m, d), out_dtype),
        compiler_params=_cparams(("parallel",)),
        name="rmsnorm",
    )(x, w.reshape(1, d))


def _panel_spec(tm, k):
    if tm > 1024:
        return pl.BlockSpec((tm, k), lambda i, j: (i, 0), pipeline_mode=pl.Buffered(1))
    return pl.BlockSpec((tm, k), lambda i, j: (i, 0))


def _norm_inputs_kernel(x_ref, xb_ref, ss_ref):
    x = x_ref[...]
    xb_ref[...] = x.astype(xb_ref.dtype)
    ss_ref[...] = jnp.broadcast_to(jnp.sum(x * x, axis=-1, keepdims=True), ss_ref.shape)


def norm_inputs(x, tm=256):
    m, d = x.shape
    return pl.pallas_call(
        _norm_inputs_kernel,
        grid=(m // tm,),
        in_specs=[pl.BlockSpec((tm, d), lambda i: (i, 0))],
        out_specs=[pl.BlockSpec((tm, d), lambda i: (i, 0)), pl.BlockSpec((tm, LANE), lambda i: (i, 0))],
        out_shape=[jax.ShapeDtypeStruct((m, d), BF16), jax.ShapeDtypeStruct((m, LANE), F32)],
        compiler_params=_cparams(("parallel",)),
        name="norm_inputs",
    )(x)


def _row_scale(ss_ref, d):
    return lax.rsqrt(ss_ref[...] / d + NORM_EPS)


def _emit_norm_inputs(x_new, xb_ref, ss_ref, ss_scr):
    j = pl.program_id(1)
    xb_ref[...] = x_new.astype(xb_ref.dtype)
    sq = x_new * x_new
    part = sq[:, :LANE]
    for c0 in range(LANE, sq.shape[1], LANE):
        part = part + sq[:, c0:c0 + LANE]

    @pl.when(j == 0)
    def _():
        ss_scr[...] = part

    @pl.when(j > 0)
    def _():
        ss_scr[...] += part

    @pl.when(j == pl.num_programs(1) - 1)
    def _():
        ss_ref[...] = jnp.broadcast_to(jnp.sum(ss_scr[...], axis=-1, keepdims=True), ss_ref.shape)


def _tile_dispatch(tile, tn, last_width):
    if last_width == tn:
        tile(tn)
    else:
        is_last = pl.program_id(1) == pl.num_programs(1) - 1
        pl.when(jnp.logical_not(is_last))(lambda: tile(tn))
        pl.when(is_last)(lambda: tile(last_width))


def _mm_in_kernel(a_ref, ss_ref, nw_ref, bt_ref, o_ref, *, last_width):
    r = _row_scale(ss_ref, a_ref.shape[1])

    def tile(width):
        bt = (bt_ref[:width, :] * nw_ref[...]).astype(BF16)
        acc = lax.dot_general(a_ref[...], bt, _NT, preferred_element_type=F32)
        for c0 in range(0, width, LANE):
            o_ref[:, c0:c0 + LANE] = acc[:, c0:c0 + LANE] * r

    _tile_dispatch(tile, o_ref.shape[1], last_width)


def matmul_in_proj(xb, ss, norm_w, w_in_t, layer, n_main, n_ab, n_pool, tm):
    m, k = xb.shape
    _, n, _ = w_in_t.shape
    main_blocks = n_main // IN_TILE
    ab_block = main_blocks + n_pool // IN_TILE

    def src_row(j):
        return jnp.where(j < main_blocks, j * IN_TILE,
                         jnp.where(j < ab_block, n_main + n_ab + (j - main_blocks) * IN_TILE, n_main))

    assert n_main + IN_TILE <= n
    return pl.pallas_call(
        functools.partial(_mm_in_kernel, last_width=LANE),
        grid=(m // tm, ab_block + 1),
        in_specs=[_panel_spec(tm, k),
                  pl.BlockSpec((tm, LANE), lambda i, j: (i, 0)),
                  pl.BlockSpec((1, k), lambda i, j: (0, 0)),
                  pl.BlockSpec((pl.Element(IN_TILE), pl.Element(k)),
                               lambda i, j: (pl.multiple_of(layer * n + src_row(j), SUBLANE), 0))],
        out_specs=pl.BlockSpec((tm, IN_TILE), lambda i, j: (i, j)),
        out_shape=jax.ShapeDtypeStruct((m, ab_block * IN_TILE + LANE), F32),
        compiler_params=_cparams(("parallel", "arbitrary")),
        name="in_proj",
    )(xb, ss, norm_w.reshape(1, k), w_in_t.reshape(-1, k))


def _norm_out_specs(tm, tn):
    return [pl.BlockSpec((tm, tn), lambda i, j: (i, j)),
            pl.BlockSpec((tm, tn), lambda i, j: (i, j)),
            pl.BlockSpec((tm, LANE), lambda i, j: (i, 0))]


def _norm_out_shapes(m, n):
    return [jax.ShapeDtypeStruct((m, n), F32), jax.ShapeDtypeStruct((m, n), BF16),
            jax.ShapeDtypeStruct((m, LANE), F32)]


def _mm_res_kernel(a_ref, b_ref, r_ref, o_ref, *norm_refs):
    x_new = r_ref[...] + jnp.dot(a_ref[...], b_ref[...], preferred_element_type=F32)
    o_ref[...] = x_new
    if norm_refs:
        _emit_norm_inputs(x_new, *norm_refs)


def matmul_residual(a, b, res, tm, tn, emit_norm_inputs):
    m, k = a.shape
    n = b.shape[1]
    x_spec = pl.BlockSpec((tm, tn), lambda i, j: (i, j))
    return pl.pallas_call(
        _mm_res_kernel,
        grid=(m // tm, n // tn),
        in_specs=[pl.BlockSpec((tm, k), lambda i, j: (i, 0)),
                  pl.BlockSpec((k, tn), lambda i, j: (0, j)),
                  x_spec],
        out_specs=_norm_out_specs(tm, tn) if emit_norm_inputs else x_spec,
        out_shape=_norm_out_shapes(m, n) if emit_norm_inputs else jax.ShapeDtypeStruct((m, n), F32),
        scratch_shapes=[pltpu.VMEM((tm, LANE), F32)] if emit_norm_inputs else [],
        compiler_params=_cparams(("parallel", "arbitrary")),
        name="matmul_residual",
    )(a, b, res)


def _layer_tile_spec(k, tn, layer):
    return pl.BlockSpec((None, k, tn), lambda i, j: (layer, 0, j))


def _mm4_res_kernel(a0, a1, a2, a3, b_ref, r_ref, o_ref, xb_ref, ss_ref, ss_scr):
    kg = a0.shape[1]
    acc = r_ref[...]
    for g, a in enumerate((a0, a1, a2, a3)):
        acc = acc + jnp.dot(a[...], b_ref[g * kg:(g + 1) * kg, :].astype(BF16), preferred_element_type=F32)
    o_ref[...] = acc
    _emit_norm_inputs(acc, xb_ref, ss_ref, ss_scr)


def matmul4_residual(parts, w_stacked, layer, res, tm, tn):
    m, kg = parts[0].shape
    _, k, n = w_stacked.shape
    a_spec = pl.BlockSpec((tm, kg), lambda i, j: (i, 0))
    return pl.pallas_call(
        _mm4_res_kernel,
        grid=(m // tm, n // tn),
        in_specs=[a_spec, a_spec, a_spec, a_spec,
                  _layer_tile_spec(k, tn, layer),
                  pl.BlockSpec((tm, tn), lambda i, j: (i, j))],
        out_specs=_norm_out_specs(tm, tn),
        out_shape=_norm_out_shapes(m, n),
        scratch_shapes=[pltpu.VMEM((tm, LANE), F32)],
        compiler_params=_cparams(("parallel", "arbitrary")),
        name="out_proj",
    )(*parts, w_stacked, res)


def _mm_glu_kernel(a_ref, ss_ref, nw_ref, bg_ref, bu_ref, o_ref, *, last_width):
    r = _row_scale(ss_ref, a_ref.shape[1])

    def scaled(b_ref, width):
        return jnp.concatenate([b_ref[:, c0:c0 + LANE] * nw_ref[...] for c0 in range(0, width, LANE)],
                               axis=1).astype(BF16)

    def tile(width):
        a = a_ref[...]
        g = jnp.dot(a, scaled(bg_ref, width), preferred_element_type=F32)
        u = jnp.dot(a, scaled(bu_ref, width), preferred_element_type=F32)
        for c0 in range(0, width, LANE):
            gate = g[:, c0:c0 + LANE] * r
            o_ref[:, c0:c0 + LANE] = (_silu(gate) * (u[:, c0:c0 + LANE] * r)).astype(o_ref.dtype)

    _tile_dispatch(tile, o_ref.shape[1], last_width)


def matmul_swiglu(xb, ss, norm_w, wg_stacked, wu_stacked, layer, tm, tn):
    m, k = xb.shape
    n = wg_stacked.shape[2]
    b_spec = _layer_tile_spec(k, tn, layer)
    last_width = n - (pl.cdiv(n, tn) - 1) * tn
    return pl.pallas_call(
        functools.partial(_mm_glu_kernel, last_width=last_width),
        grid=(m // tm, pl.cdiv(n, tn)),
        in_specs=[pl.BlockSpec((tm, k), lambda i, j: (i, 0), pipeline_mode=pl.Buffered(1)),
                  pl.BlockSpec((tm, LANE), lambda i, j: (i, 0)),
                  pl.BlockSpec((k, LANE), lambda i, j: (0, 0)),
                  b_spec, b_spec],
        out_specs=pl.BlockSpec((tm, tn), lambda i, j: (i, j)),
        out_shape=jax.ShapeDtypeStruct((m, n), BF16),
        compiler_params=_cparams(("parallel", "arbitrary")),
        name="ffn_gate_up",
    )(xb, ss, jnp.broadcast_to(norm_w.reshape(k, 1), (k, LANE)), wg_stacked, wu_stacked)


def _halo_spec(ts, hb, c, col):
    r = ts // hb
    return pl.BlockSpec((hb, c), lambda i: (jnp.maximum(i * r - 1, 0), col))


def _main_spec(ts, c, col):
    return pl.BlockSpec((ts, c), lambda i: (i, col))


def _fill_history(scr, first, halo, cur, hb):
    scr[0:hb, :] = jnp.where(first, 0.0, halo)
    scr[hb:, :] = cur


def _mix_sconv_kernel(b_ref, c_ref, h_ref, ch_ref, hh_ref, w_ref, o_ref, scr, *, tiles_per_seq):
    ts = o_ref.shape[0]
    first = (pl.program_id(0) % tiles_per_seq) == 0
    m = c_ref[...] * h_ref[...]
    _fill_history(scr, first, ch_ref[...] * hh_ref[...], m, SUBLANE)
    w = w_ref[...]
    kw = w.shape[0]
    y = w[kw - 1:kw, :] * m
    for d in range(1, kw):
        y = y + w[kw - 1 - d:kw - d, :] * scr[SUBLANE - d:SUBLANE - d + ts, :]
    o_ref[...] = (b_ref[...] * y).astype(o_ref.dtype)


def mix_sconv(sb, sc, sh, w, seq, ts=256):
    m = sb[0].shape[0]
    c = w.shape[1]
    kern = functools.partial(_mix_sconv_kernel, tiles_per_seq=seq // ts)
    return pl.pallas_call(
        kern,
        grid=(m // ts,),
        in_specs=[_main_spec(ts, c, sb[1]), _main_spec(ts, c, sc[1]), _main_spec(ts, c, sh[1]),
                  _halo_spec(ts, SUBLANE, c, sc[1]), _halo_spec(ts, SUBLANE, c, sh[1]),
                  pl.BlockSpec(w.shape, lambda i: (0, 0))],
        out_specs=pl.BlockSpec((ts, c), lambda i: (i, 0)),
        out_shape=jax.ShapeDtypeStruct((m, c), BF16),
        scratch_shapes=[pltpu.VMEM((SUBLANE + ts, c), F32)],
        compiler_params=_cparams(("parallel",)),
        name="mix_sconv",
    )(sb[0], sc[0], sh[0], sc[0], sh[0], w)


CF_HALO = 32


def _mix_conformer_kernel(v_ref, g_ref, vh_ref, gh_ref, w_ref, cb_ref, lw_ref, lb_ref, o_ref,
                          scr, sh_scr, cf_scr, *, tiles_per_seq):
    ts, c = o_ref.shape
    first = (pl.program_id(0) % tiles_per_seq) == 0
    glu = v_ref[...] * _sigmoid(g_ref[...])
    _fill_history(scr, first, vh_ref[...] * _sigmoid(gh_ref[...]), glu, CF_HALO)
    n_sh = sh_scr.shape[1]
    for r in range(1, SUBLANE):
        sh_scr[r - 1] = scr[SUBLANE - r:SUBLANE - r + n_sh, :]
    kw = w_ref.shape[0]
    s1 = jnp.zeros((ts, LANE), F32)
    for c0 in range(0, c, LANE):
        acc = jnp.zeros((ts, LANE), F32) + cb_ref[:, c0:c0 + LANE]
        for d in range(kw):
            a, r = divmod(d, SUBLANE)
            if r == 0:
                tap = scr[CF_HALO - d:CF_HALO - d + ts, c0:c0 + LANE]
            else:
                off = CF_HALO - SUBLANE * (a + 1)
                tap = sh_scr[r - 1, off:off + ts, c0:c0 + LANE]
            acc = acc + w_ref[kw - 1 - d:kw - d, c0:c0 + LANE] * tap
        cf_scr[:, c0:c0 + LANE] = acc
        s1 = s1 + acc
    mu = jnp.sum(s1, axis=-1, keepdims=True) * (1.0 / c)
    s2 = jnp.zeros((ts, LANE), F32)
    for c0 in range(0, c, LANE):
        xc = cf_scr[:, c0:c0 + LANE] - mu
        s2 = s2 + xc * xc
    rstd = lax.rsqrt(jnp.sum(s2, axis=-1, keepdims=True) * (1.0 / c) + NORM_EPS)
    for c0 in range(0, c, LANE):
        y = (cf_scr[:, c0:c0 + LANE] - mu) * rstd * lw_ref[:, c0:c0 + LANE] + lb_ref[:, c0:c0 + LANE]
        o_ref[:, c0:c0 + LANE] = _silu(y).astype(o_ref.dtype)


def mix_conformer(sv, sg, w, cb, lw, lb, seq, ts=128):
    m = sv[0].shape[0]
    c = w.shape[1]
    kern = functools.partial(_mix_conformer_kernel, tiles_per_seq=seq // ts)
    vec = pl.BlockSpec((1, c), lambda i: (0, 0))
    return pl.pallas_call(
        kern,
        grid=(m // ts,),
        in_specs=[_main_spec(ts, c, sv[1]), _main_spec(ts, c, sg[1]),
                  _halo_spec(ts, CF_HALO, c, sv[1]), _halo_spec(ts, CF_HALO, c, sg[1]),
                  pl.BlockSpec(w.shape, lambda i: (0, 0)), vec, vec, vec],
        out_specs=pl.BlockSpec((ts, c), lambda i: (i, 0)),
        out_shape=jax.ShapeDtypeStruct((m, c), BF16),
        scratch_shapes=[pltpu.VMEM((CF_HALO + ts, c), F32),
                        pltpu.VMEM((SUBLANE - 1, CF_HALO - SUBLANE + ts, c), F32),
                        pltpu.VMEM((ts, c), F32)],
        compiler_params=_cparams(("parallel",)),
        name="mix_conformer",
    )(sv[0], sg[0], sv[0], sg[0], w, cb.reshape(1, c), lw.reshape(1, c), lb.reshape(1, c))


POOL_HALO = 16


def _mix_pool_kernel(u_ref, uh_ref, pw_ref, ps_ref, o_ref, scr, *, tiles_per_seq):
    ts, c = o_ref.shape
    ng = pw_ref.shape[0]
    cg = c // ng
    tile = pl.program_id(0) % tiles_per_seq
    first = tile == 0
    _fill_history(scr, first, uh_ref[...], u_ref[...], POOL_HALO)
    pos = tile * ts + lax.broadcasted_iota(jnp.int32, (ts, 1), 0)
    for gi, win in enumerate(POOL_WINDOWS):
        c0 = gi * cg
        cur = scr[POOL_HALO:POOL_HALO + ts, c0:c0 + cg]
        acc = cur
        for d in range(1, win):
            acc = acc + scr[POOL_HALO - d:POOL_HALO - d + ts, c0:c0 + cg]
        cnt = jnp.minimum(pos + 1, win).astype(F32)
        p = acc / cnt - cur
        y = _dot_bf16(p, pw_ref[gi])
        o_ref[:, c0:c0 + cg] = (y * ps_ref[:, c0:c0 + cg]).astype(o_ref.dtype)


def mix_pool(su, pool_w, pool_scale, seq, ts=256):
    m = su[0].shape[0]
    ng, cg, _ = pool_w.shape
    c = ng * cg
    kern = functools.partial(_mix_pool_kernel, tiles_per_seq=seq // ts)
    return pl.pallas_call(
        kern,
        grid=(m // ts,),
        in_specs=[_main_spec(ts, c, su[1]), _halo_spec(ts, POOL_HALO, c, su[1]),
                  pl.BlockSpec(pool_w.shape, lambda i: (0, 0, 0)),
                  pl.BlockSpec((1, c), lambda i: (0, 0))],
        out_specs=pl.BlockSpec((ts, c), lambda i: (i, 0)),
        out_shape=jax.ShapeDtypeStruct((m, c), BF16),
        scratch_shapes=[pltpu.VMEM((POOL_HALO + ts, c), F32)],
        compiler_params=_cparams(("parallel",)),
        name="mix_pool",
    )(su[0], su[0], pool_w.astype(BF16), pool_scale.reshape(1, c))


def _split3(x):
    x1 = x.astype(BF16)
    r1 = x - x1.astype(F32)
    x2 = r1.astype(BF16)
    x3 = (r1 - x2.astype(F32)).astype(BF16)
    return x1, x2, x3


P1_GROUP = 16


def _gdn_kernel(q_ref, k_ref, v_ref, qh_ref, kh_ref, vh_ref, z_ref, ab_ref,
                wq_ref, wk_ref, wv_ref, alog_ref, dtb_ref, nw_ref, o_ref,
                st_ref, hist, qs, ks, vs, gc_s, gcr_s, kn_s, kbq_s, vk_s, wqd_s, kd_s, u_s, a_s,
                *, n_heads):
    ts = o_ref.shape[0]
    dh = GDN_HEAD_DIM
    cc = GDN_CHUNK
    n_chunks = ts // cc
    first = pl.program_id(1) == 0

    @pl.when(first)
    def _():
        st_ref[...] = jnp.zeros_like(st_ref)

    def conv_silu(x_ref, xh_ref, w_ref, dst):
        _fill_history(hist, first, xh_ref[...], x_ref[...], SUBLANE)
        kw = w_ref.shape[0]
        acc = w_ref[kw - 1:kw, :] * x_ref[...]
        for d in range(1, kw):
            acc = acc + w_ref[kw - 1 - d:kw - d, :] * hist[SUBLANE - d:SUBLANE - d + ts, :]
        dst[...] = _silu(acc)

    conv_silu(q_ref, qh_ref, wq_ref, qs)
    conv_silu(k_ref, kh_ref, wk_ref, ks)
    conv_silu(v_ref, vh_ref, wv_ref, vs)

    ab = ab_ref[...]
    xg = ab + dtb_ref[...]
    softplus = jnp.maximum(xg, 0.0) + jnp.log1p(jnp.exp(-jnp.abs(xg)))
    g_all = -jnp.exp(alog_ref[...]) * softplus
    beta_all = _sigmoid(ab)

    rt = lax.broadcasted_iota(jnp.int32, (ts, ts), 0)
    ct = lax.broadcasted_iota(jnp.int32, (ts, ts), 1)
    tri_blk = jnp.where((rt >= ct) & ((rt // cc) == (ct // cc)), 1.0, 0.0).astype(BF16)
    g1, g2, g3 = _split3(g_all)
    gc_all = (jnp.dot(tri_blk, g1, preferred_element_type=F32)
              + jnp.dot(tri_blk, g2, preferred_element_type=F32)
              + jnp.dot(tri_blk, g3, preferred_element_type=F32))
    gc_s[...] = gc_all
    gcr_s[...] = gc_all.T

    scale = dh ** -0.5

    for h in range(n_heads):
        lanes = slice(h * dh, (h + 1) * dh)
        gc_col = jnp.broadcast_to(gc_all[:, h:h + 1], (ts, dh))
        b_col = jnp.broadcast_to(beta_all[:, n_heads + h:n_heads + h + 1], (ts, dh))
        q_all = qs[:, lanes]
        k_all = ks[:, lanes]
        q_all = q_all * (lax.rsqrt(jnp.sum(q_all * q_all, axis=-1, keepdims=True) + NORM_EPS) * scale)
        k_all = k_all * lax.rsqrt(jnp.sum(k_all * k_all, axis=-1, keepdims=True) + NORM_EPS)
        egc_all = jnp.exp(gc_col)
        kb_all = k_all * b_col
        kn_s[:, lanes] = k_all.astype(BF16)
        vk_s[:, 2 * h * dh:(2 * h + 1) * dh] = (vs[:, lanes] * b_col).astype(BF16)
        vk_s[:, (2 * h + 1) * dh:(2 * h + 2) * dh] = (kb_all * egc_all).astype(BF16)
        qd_all = q_all * egc_all
        for c in range(n_chunks):
            rows = slice(c * cc, (c + 1) * cc)
            gc_last = gc_col[(c + 1) * cc - 1:(c + 1) * cc, :]
            kbq_s[2 * c * cc:(2 * c + 1) * cc, lanes] = kb_all[rows].astype(BF16)
            kbq_s[(2 * c + 1) * cc:(2 * c + 2) * cc, lanes] = q_all[rows].astype(BF16)
            wqd_s[(2 * c + 1) * cc:(2 * c + 2) * cc, lanes] = qd_all[rows].astype(BF16)
            kd_s[rows, lanes] = (k_all[rows] * jnp.exp(gc_last - gc_col[rows])).astype(BF16)

    row = lax.broadcasted_iota(jnp.int32, (cc, cc), 0)
    col = lax.broadcasted_iota(jnp.int32, (cc, cc), 1)
    eye = (row == col).astype(F32)

    def p1_stages(c, h):
        rows = slice(c * cc, (c + 1) * cc)
        lanes = slice(h * dh, (h + 1) * dh)
        s = {}

        def s0():
            s["kq"] = lax.dot_general(kbq_s[2 * c * cc:(2 * c + 2) * cc, lanes], kn_s[rows, lanes], _NT,
                                      preferred_element_type=F32)

        def s1():
            pdiff = jnp.broadcast_to(gc_s[rows, h:h + 1], (cc, cc)) - gcr_s[h:h + 1, rows]
            decay = jnp.exp(jnp.where(row >= col, pdiff, -jnp.inf))
            kq = s.pop("kq")
            s["m"] = jnp.where(row > col, kq[:cc] * decay, 0.0)
            a_s[rows, h * cc:(h + 1) * cc] = (kq[cc:] * decay).astype(BF16)
            s["m1"] = s["m"].astype(BF16)
            s["p2"] = _dot_bf16(s["m1"], s["m1"])

        def s2():
            s["p2b"] = s["p2"].astype(BF16)
            s["p4"] = _dot_bf16(s["p2b"], s["p2b"])
            s["mp2"] = _dot_bf16(s.pop("m1"), s["p2b"])

        def s3():
            s["p4b"] = s["p4"].astype(BF16)
            s["p8"] = _dot_bf16(s["p4b"], s["p4b"])
            s["x1"] = eye - s.pop("m") + s.pop("p2") - s.pop("mp2")
            s.pop("p2b")

        def s4():
            p8b = s["p8"].astype(BF16)
            s["p16"] = _dot_bf16(p8b, p8b)
            s["p12"] = _dot_bf16(s.pop("p4b"), p8b)

        def s5():
            s["p16b"] = s["p16"].astype(BF16)
            s["p32"] = _dot_bf16(s["p16b"], s["p16b"])
            x2 = eye + s.pop("p4") + s.pop("p8") + s.pop("p12")
            s["x12"] = _dot_bf16(s.pop("x1"), x2)

        def s6():
            s["p48"] = _dot_bf16(s.pop("p16b"), s["p32"])

        def s7():
            x3 = eye + s.pop("p16") + s.pop("p32") + s.pop("p48")
            s["t"] = _dot_bf16(s.pop("x12"), x3)

        def s8():
            uw = _dot_bf16(s.pop("t"), vk_s[rows, 2 * h * dh:(2 * h + 2) * dh])
            u_s[rows, lanes] = uw[:, :dh]
            wqd_s[2 * c * cc:(2 * c + 1) * cc, lanes] = uw[:, dh:].astype(BF16)

        return (s0, s1, s2, s3, s4, s5, s6, s7, s8)

    chains = [(c, h) for c in range(n_chunks) for h in range(n_heads)]
    for g0 in range(0, len(chains), P1_GROUP):
        group = [p1_stages(c, h) for c, h in chains[g0:g0 + P1_GROUP]]
        for stage in zip(*group):
            for fn in stage:
                fn()

    for c in range(n_chunks):
        rows = slice(c * cc, (c + 1) * cc)
        states = [st_ref[h] for h in range(n_heads)]
        ws_qs = [lax.dot_general(wqd_s[2 * c * cc:(2 * c + 2) * cc, h * dh:(h + 1) * dh],
                                 states[h].astype(BF16), _NN, preferred_element_type=F32)
                 for h in range(n_heads)]
        v_new = [(u_s[rows, h * dh:(h + 1) * dh] - ws_qs[h][:cc]).astype(BF16) for h in range(n_heads)]
        o_intra = [lax.dot_general(a_s[rows, h * cc:(h + 1) * cc], v_new[h], _NN, preferred_element_type=F32)
                   for h in range(n_heads)]
        s_upd = [lax.dot_general(kd_s[rows, h * dh:(h + 1) * dh], v_new[h], _TN, preferred_element_type=F32)
                 for h in range(n_heads)]
        for h in range(n_heads):
            lanes = slice(h * dh, (h + 1) * dh)
            g_last = jnp.exp(jnp.broadcast_to(gc_s[(c + 1) * cc - 1:(c + 1) * cc, h:h + 1], (1, dh)))
            st_ref[h] = states[h] * g_last + s_upd[h]
            o = ws_qs[h][cc:] + o_intra[h]
            o = o * lax.rsqrt(jnp.mean(o * o, axis=-1, keepdims=True) + NORM_EPS) * nw_ref[...]
            o_ref[rows, lanes] = (o * _silu(z_ref[rows, lanes])).astype(o_ref.dtype)


def mix_gdn(sq, sk, sv, sz, sab, conv_w, a_log, dt_bias, norm_w, batch, seq, ts=256):
    m = sq[0].shape[0]
    c = conv_w.shape[1] // 3
    n_heads = c // GDN_HEAD_DIM
    nt = seq // ts
    r = ts // SUBLANE

    def main(colblk):
        return pl.BlockSpec((ts, c), lambda b, s: (b * nt + s, colblk))

    def halo(colblk):
        return pl.BlockSpec((SUBLANE, c), lambda b, s: (jnp.maximum((b * nt + s) * r - 1, 0), colblk))

    def wspec(j):
        return pl.BlockSpec((conv_w.shape[0], c), lambda b, s: (0, j))

    vec = pl.BlockSpec((1, LANE), lambda b, s: (0, 0))
    pad = LANE - n_heads
    alog = jnp.pad(a_log, (0, pad)).reshape(1, LANE)
    dtb = jnp.pad(dt_bias, (0, pad)).reshape(1, LANE)
    kern = functools.partial(_gdn_kernel, n_heads=n_heads)
    return pl.pallas_call(
        kern,
        grid=(batch, nt),
        in_specs=[main(sq[1]), main(sk[1]), main(sv[1]),
                  halo(sq[1]), halo(sk[1]), halo(sv[1]),
                  main(sz[1]),
                  pl.BlockSpec((ts, LANE), lambda b, s: (b * nt + s, sab[1])),
                  wspec(0), wspec(1), wspec(2), vec, vec, vec],
        out_specs=pl.BlockSpec((ts, c), lambda b, s: (b * nt + s, 0)),
        out_shape=jax.ShapeDtypeStruct((m, c), BF16),
        scratch_shapes=[pltpu.VMEM((n_heads, GDN_HEAD_DIM, GDN_HEAD_DIM), F32),
                        pltpu.VMEM((SUBLANE + ts, c), F32),
                        pltpu.VMEM((ts, c), F32),
                        pltpu.VMEM((ts, c), F32),
                        pltpu.VMEM((ts, c), F32),
                        pltpu.VMEM((ts, LANE), F32),
                        pltpu.VMEM((LANE, ts), F32),
                        pltpu.VMEM((ts, c), BF16),
                        pltpu.VMEM((2 * ts, c), BF16),
                        pltpu.VMEM((ts, 2 * c), BF16),
                        pltpu.VMEM((2 * ts, c), BF16),
                        pltpu.VMEM((ts, c), BF16),
                        pltpu.VMEM((ts, c), F32),
                        pltpu.VMEM((ts, n_heads * GDN_CHUNK), BF16)],
        compiler_params=_cparams(("arbitrary", "arbitrary")),
        name="mix_gdn",
    )(sq[0], sk[0], sv[0], sq[0], sk[0], sv[0], sz[0], sab[0],
      conv_w, conv_w, conv_w, alog, dtb, norm_w.reshape(1, LANE))


FF_TILE = 512


def _layer(x2, xb, ss, batch, seq, li, p, stacked, last):
    cg = p["sc_conv_w"].shape[1]
    n_heads = cg // GDN_HEAD_DIM
    n_main = 9 * cg
    n_ab = 2 * n_heads

    proj = matmul_in_proj(xb, ss, p["attn_norm_w"], stacked["w_in_t"], li, n_main, n_ab, cg, tm=2048)

    y_a = mix_sconv((proj, 0), (proj, 1), (proj, 2), p["sc_conv_w"], seq)
    y_b = mix_conformer((proj, 3), (proj, 4), p["cf_conv_w"], p["cf_conv_b"], p["cf_ln_w"], p["cf_ln_b"], seq)
    y_c = mix_gdn((proj, 5), (proj, 6), (proj, 7), (proj, 8), (proj, (n_main + cg) // LANE),
                  p["gdn_conv_w"], p["gdn_a_log"], p["gdn_dt_bias"], p["gdn_norm_w"], batch, seq)
    y_d = mix_pool((proj, 9), p["pool_w"], p["pool_scale"], seq)

    x2, xb, ss = matmul4_residual((y_a, y_b, y_c, y_d), stacked["w_out"], li, x2, tm=1024, tn=512)

    wd = cast_weight(stacked["w_down"], li)
    act = matmul_swiglu(xb, ss, p["ffn_norm_w"], stacked["w_gate"], stacked["w_up"], li, tm=1024, tn=FF_TILE)
    if last:
        return matmul_residual(act, wd, x2, tm=512, tn=512, emit_norm_inputs=False), None, None
    return matmul_residual(act, wd, x2, tm=512, tn=512, emit_norm_inputs=True)


def kernel(x, attn_norm_w, w_in, sc_conv_w, cf_conv_w, cf_conv_b, cf_ln_w, cf_ln_b, gdn_conv_w,
           gdn_a_log, gdn_dt_bias, gdn_norm_w, pool_w, pool_scale, w_out, ffn_norm_w, w_gate, w_up,
           w_down, final_norm_w):
    batch, seq, d = x.shape
    small = dict(attn_norm_w=attn_norm_w, sc_conv_w=sc_conv_w, cf_conv_w=cf_conv_w,
                 cf_conv_b=cf_conv_b, cf_ln_w=cf_ln_w, cf_ln_b=cf_ln_b, gdn_conv_w=gdn_conv_w,
                 gdn_a_log=gdn_a_log, gdn_dt_bias=gdn_dt_bias, gdn_norm_w=gdn_norm_w, pool_w=pool_w,
                 pool_scale=pool_scale, ffn_norm_w=ffn_norm_w)
    stacked = dict(w_in_t=jnp.swapaxes(w_in, 1, 2), w_out=w_out, w_gate=w_gate, w_up=w_up, w_down=w_down)
    x2 = x.reshape(batch * seq, d)
    xb, ss = norm_inputs(x2)
    depth = attn_norm_w.shape[0]
    for li in range(depth):
        x2, xb, ss = _layer(x2, xb, ss, batch, seq, li, {name: val[li] for name, val in small.items()},
                            stacked, last=li == depth - 1)
    out = rmsnorm(x2, final_norm_w, x.dtype)
    return out.reshape(batch, seq, d)
```

```python
import functools

import jax
import jax.numpy as jnp
from jax import lax
from jax.experimental import pallas as pl
from jax.experimental.pallas import tpu as pltpu

NORM_EPS = 1e-6
GDN_HEAD_DIM = 128
GDN_CHUNK = 64
POOL_WINDOWS = (2, 4, 8, 16)
LANE = 128
SUBLANE = 8
VMEM_LIMIT = 56 * 1024 * 1024

IN_ROWS, IN_TILE = 2048, 512
OUT_ROWS, OUT_TILE = 1024, 512
FF_ROWS, FF_TILE = 1024, 512
DOWN_ROWS, DOWN_TILE = 512, 512

BF16 = jnp.bfloat16
F32 = jnp.float32


def _cparams(sem):
    return pltpu.CompilerParams(dimension_semantics=sem, vmem_limit_bytes=VMEM_LIMIT)


def _sigmoid(x):
    return 1.0 / (1.0 + jnp.exp(-x))


def _silu(x):
    return x * _sigmoid(x)


_NN = (((1,), (0,)), ((), ()))
_NT = (((1,), (1,)), ((), ()))
_TN = (((0,), (0,)), ((), ()))


def _dot_bf16(a, b, dims=_NN):
    return lax.dot_general(a.astype(BF16), b.astype(BF16), dims, preferred_element_type=F32)


def _cast_kernel(x_ref, o_ref):
    o_ref[...] = x_ref[...].astype(o_ref.dtype)


def cast_weight(w_stacked, layer, tk=128):
    _, k, n = w_stacked.shape
    return pl.pallas_call(
        _cast_kernel,
        grid=(k // tk,),
        in_specs=[pl.BlockSpec((None, tk, n), lambda r: (layer, r, 0))],
        out_specs=pl.BlockSpec((tk, n), lambda r: (r, 0)),
        out_shape=jax.ShapeDtypeStruct((k, n), BF16),
        compiler_params=_cparams(("parallel",)),
        name="cast_weight",
    )(w_stacked)


def _rmsnorm_kernel(x_ref, w_ref, o_ref):
    x = x_ref[...]
    ms = jnp.mean(x * x, axis=-1, keepdims=True)
    o_ref[...] = (x * lax.rsqrt(ms + NORM_EPS) * w_ref[...]).astype(o_ref.dtype)


def rmsnorm(x, w, out_dtype, tm=256):
    m, d = x.shape
    return pl.pallas_call(
        _rmsnorm_kernel,
        grid=(m // tm,),
        in_specs=[pl.BlockSpec((tm, d), lambda i: (i, 0)),
                  pl.BlockSpec((1, d), lambda i: (0, 0))],
        out_specs=pl.BlockSpec((tm, d), lambda i: (i, 0)),
        out_shape=jax.ShapeDtypeStruct((m, d), out_dtype),
        compiler_params=_cparams(("parallel",)),
        name="rmsnorm",
    )(x, w.reshape(1, d))


def _panel_spec(tm, k):
    return pl.BlockSpec((tm, k), lambda i, j: (i, 0), pipeline_mode=pl.Buffered(1))


def _norm_inputs_kernel(x_ref, xb_ref, ss_ref):
    x = x_ref[...]
    xb_ref[...] = x.astype(xb_ref.dtype)
    ss_ref[...] = jnp.broadcast_to(jnp.sum(x * x, axis=-1, keepdims=True), ss_ref.shape)


def norm_inputs(x, tm=256):
    m, d = x.shape
    return pl.pallas_call(
        _norm_inputs_kernel,
        grid=(m // tm,),
        in_specs=[pl.BlockSpec((tm, d), lambda i: (i, 0))],
        out_specs=[pl.BlockSpec((tm, d), lambda i: (i, 0)), pl.BlockSpec((tm, LANE), lambda i: (i, 0))],
        out_shape=[jax.ShapeDtypeStruct((m, d), BF16), jax.ShapeDtypeStruct((m, LANE), F32)],
        compiler_params=_cparams(("parallel",)),
        name="norm_inputs",
    )(x)


def _row_scale(ss_ref, d):
    return lax.rsqrt(ss_ref[...] / d + NORM_EPS)


def _reset_row_sums(ss_scr):
    @pl.when(pl.program_id(1) == 0)
    def _():
        ss_scr[...] = jnp.zeros_like(ss_scr)


def _emit_norm_inputs(x_new, xb_ref, ss_ref, ss_scr):
    xb_ref[...] = x_new.astype(xb_ref.dtype)
    sq = x_new * x_new
    part = ss_scr[...]
    for c0 in range(0, sq.shape[1], LANE):
        part = part + sq[:, c0:c0 + LANE]
    ss_scr[...] = part
    ss_ref[...] = jnp.broadcast_to(jnp.sum(part, axis=-1, keepdims=True), ss_ref.shape)


def _tile_dispatch(tile, tn, last_width):
    if last_width == tn:
        tile(tn)
    else:
        is_last = pl.program_id(1) == pl.num_programs(1) - 1
        pl.when(jnp.logical_not(is_last))(lambda: tile(tn))
        pl.when(is_last)(lambda: tile(last_width))


def _mm_in_kernel(a_ref, ss_ref, nw_ref, bt_ref, o_ref, *, last_width):
    r = _row_scale(ss_ref, a_ref.shape[1])

    def tile(width):
        bt = (bt_ref[:width, :] * nw_ref[...]).astype(BF16)
        acc = lax.dot_general(a_ref[...], bt, _NT, preferred_element_type=F32)
        for c0 in range(0, width, LANE):
            o_ref[:, c0:c0 + LANE] = acc[:, c0:c0 + LANE] * r

    _tile_dispatch(tile, o_ref.shape[1], last_width)


def matmul_in_proj(xb, ss, norm_w, w_in_t, layer, n_main, n_ab, n_pool, tm):
    m, k = xb.shape
    _, n, _ = w_in_t.shape
    main_blocks = n_main // IN_TILE
    ab_block = main_blocks + n_pool // IN_TILE

    def src_row(j):
        return jnp.where(j < main_blocks, j * IN_TILE,
                         jnp.where(j < ab_block, n_main + n_ab + (j - main_blocks) * IN_TILE, n_main))

    assert n_main + IN_TILE <= n
    return pl.pallas_call(
        functools.partial(_mm_in_kernel, last_width=LANE),
        grid=(m // tm, ab_block + 1),
        in_specs=[_panel_spec(tm, k),
                  pl.BlockSpec((tm, LANE), lambda i, j: (i, 0)),
                  pl.BlockSpec((1, k), lambda i, j: (0, 0)),
                  pl.BlockSpec((pl.Element(IN_TILE), pl.Element(k)),
                               lambda i, j: (pl.multiple_of(layer * n + src_row(j), SUBLANE), 0))],
        out_specs=pl.BlockSpec((tm, IN_TILE), lambda i, j: (i, j)),
        out_shape=jax.ShapeDtypeStruct((m, ab_block * IN_TILE + LANE), F32),
        compiler_params=_cparams(("parallel", "arbitrary")),
        name="in_proj",
    )(xb, ss, norm_w.reshape(1, k), w_in_t.reshape(-1, k))


def _norm_out_specs(tm, tn):
    return [pl.BlockSpec((tm, tn), lambda i, j: (i, j)),
            pl.BlockSpec((tm, tn), lambda i, j: (i, j)),
            pl.BlockSpec((tm, LANE), lambda i, j: (i, 0))]


def _norm_out_shapes(m, n):
    return [jax.ShapeDtypeStruct((m, n), F32), jax.ShapeDtypeStruct((m, n), BF16),
            jax.ShapeDtypeStruct((m, LANE), F32)]


def _mm_res_kernel(a_ref, b_ref, r_ref, o_ref, *norm_refs):
    if norm_refs:
        _reset_row_sums(norm_refs[-1])
    x_new = r_ref[...] + jnp.dot(a_ref[...], b_ref[...], preferred_element_type=F32)
    o_ref[...] = x_new
    if norm_refs:
        _emit_norm_inputs(x_new, *norm_refs)


def matmul_residual(a, b, res, tm, tn, emit_norm_inputs):
    m, k = a.shape
    n = b.shape[1]
    x_spec = pl.BlockSpec((tm, tn), lambda i, j: (i, j))
    return pl.pallas_call(
        _mm_res_kernel,
        grid=(m // tm, n // tn),
        in_specs=[pl.BlockSpec((tm, k), lambda i, j: (i, 0)),
                  pl.BlockSpec((k, tn), lambda i, j: (0, j)),
                  x_spec],
        out_specs=_norm_out_specs(tm, tn) if emit_norm_inputs else x_spec,
        out_shape=_norm_out_shapes(m, n) if emit_norm_inputs else jax.ShapeDtypeStruct((m, n), F32),
        scratch_shapes=[pltpu.VMEM((tm, LANE), F32)] if emit_norm_inputs else [],
        compiler_params=_cparams(("parallel", "arbitrary")),
        name="matmul_residual",
    )(a, b, res)


def _layer_tile_spec(k, tn, layer):
    return pl.BlockSpec((None, k, tn), lambda i, j: (layer, 0, j))


def _mm4_res_kernel(a0, a1, a2, a3, b_ref, r_ref, o_ref, xb_ref, ss_ref, ss_scr):
    kg = a0.shape[1]
    _reset_row_sums(ss_scr)
    acc = r_ref[...]
    for g, a in enumerate((a0, a1, a2, a3)):
        acc = acc + jnp.dot(a[...], b_ref[g * kg:(g + 1) * kg, :].astype(BF16), preferred_element_type=F32)
    o_ref[...] = acc
    _emit_norm_inputs(acc, xb_ref, ss_ref, ss_scr)


def matmul4_residual(parts, w_stacked, layer, res, tm, tn):
    m, kg = parts[0].shape
    _, k, n = w_stacked.shape
    a_spec = pl.BlockSpec((tm, kg), lambda i, j: (i, 0))
    return pl.pallas_call(
        _mm4_res_kernel,
        grid=(m // tm, n // tn),
        in_specs=[a_spec, a_spec, a_spec, a_spec,
                  _layer_tile_spec(k, tn, layer),
                  pl.BlockSpec((tm, tn), lambda i, j: (i, j))],
        out_specs=_norm_out_specs(tm, tn),
        out_shape=_norm_out_shapes(m, n),
        scratch_shapes=[pltpu.VMEM((tm, LANE), F32)],
        compiler_params=_cparams(("parallel", "arbitrary")),
        name="out_proj",
    )(*parts, w_stacked, res)


def _mm_glu_kernel(a_ref, ss_ref, nw_ref, bg_ref, bu_ref, o_ref, *, last_width):
    r = _row_scale(ss_ref, a_ref.shape[1])

    def scaled(b_ref, width):
        return jnp.concatenate([b_ref[:, c0:c0 + LANE] * nw_ref[...] for c0 in range(0, width, LANE)],
                               axis=1).astype(BF16)

    def tile(width):
        a = a_ref[...]
        g = jnp.dot(a, scaled(bg_ref, width), preferred_element_type=F32)
        u = jnp.dot(a, scaled(bu_ref, width), preferred_element_type=F32)
        for c0 in range(0, width, LANE):
            gate = g[:, c0:c0 + LANE] * r
            o_ref[:, c0:c0 + LANE] = (_silu(gate) * (u[:, c0:c0 + LANE] * r)).astype(o_ref.dtype)

    _tile_dispatch(tile, o_ref.shape[1], last_width)


def matmul_swiglu(xb, ss, norm_w, wg_stacked, wu_stacked, layer, tm, tn):
    m, k = xb.shape
    n = wg_stacked.shape[2]
    b_spec = _layer_tile_spec(k, tn, layer)
    last_width = n - (pl.cdiv(n, tn) - 1) * tn
    return pl.pallas_call(
        functools.partial(_mm_glu_kernel, last_width=last_width),
        grid=(m // tm, pl.cdiv(n, tn)),
        in_specs=[_panel_spec(tm, k),
                  pl.BlockSpec((tm, LANE), lambda i, j: (i, 0)),
                  pl.BlockSpec((k, LANE), lambda i, j: (0, 0)),
                  b_spec, b_spec],
        out_specs=pl.BlockSpec((tm, tn), lambda i, j: (i, j)),
        out_shape=jax.ShapeDtypeStruct((m, n), BF16),
        compiler_params=_cparams(("parallel", "arbitrary")),
        name="ffn_gate_up",
    )(xb, ss, jnp.broadcast_to(norm_w.reshape(k, 1), (k, LANE)), wg_stacked, wu_stacked)


def _halo_spec(ts, hb, c, col):
    r = ts // hb
    return pl.BlockSpec((hb, c), lambda i: (jnp.maximum(i * r - 1, 0), col))


def _main_spec(ts, c, col):
    return pl.BlockSpec((ts, c), lambda i: (i, col))


def _fill_history(scr, first, halo, cur, hb):
    scr[0:hb, :] = jnp.where(first, 0.0, halo)
    scr[hb:, :] = cur


def _mix_sconv_kernel(b_ref, c_ref, h_ref, ch_ref, hh_ref, w_ref, o_ref, scr, *, tiles_per_seq):
    ts = o_ref.shape[0]
    first = (pl.program_id(0) % tiles_per_seq) == 0
    m = c_ref[...] * h_ref[...]
    _fill_history(scr, first, ch_ref[...] * hh_ref[...], m, SUBLANE)
    w = w_ref[...]
    kw = w.shape[0]
    y = w[kw - 1:kw, :] * m
    for d in range(1, kw):
        y = y + w[kw - 1 - d:kw - d, :] * scr[SUBLANE - d:SUBLANE - d + ts, :]
    o_ref[...] = (b_ref[...] * y).astype(o_ref.dtype)


def mix_sconv(sb, sc, sh, w, seq, ts=256):
    m = sb[0].shape[0]
    c = w.shape[1]
    kern = functools.partial(_mix_sconv_kernel, tiles_per_seq=seq // ts)
    return pl.pallas_call(
        kern,
        grid=(m // ts,),
        in_specs=[_main_spec(ts, c, sb[1]), _main_spec(ts, c, sc[1]), _main_spec(ts, c, sh[1]),
                  _halo_spec(ts, SUBLANE, c, sc[1]), _halo_spec(ts, SUBLANE, c, sh[1]),
                  pl.BlockSpec(w.shape, lambda i: (0, 0))],
        out_specs=pl.BlockSpec((ts, c), lambda i: (i, 0)),
        out_shape=jax.ShapeDtypeStruct((m, c), BF16),
        scratch_shapes=[pltpu.VMEM((SUBLANE + ts, c), F32)],
        compiler_params=_cparams(("parallel",)),
        name="mix_sconv",
    )(sb[0], sc[0], sh[0], sc[0], sh[0], w)


CF_HALO = 32


def _mix_conformer_kernel(v_ref, g_ref, vh_ref, gh_ref, w_ref, cb_ref, lw_ref, lb_ref, o_ref,
                          scr, sh_scr, cf_scr, *, tiles_per_seq):
    ts, c = o_ref.shape
    first = (pl.program_id(0) % tiles_per_seq) == 0
    glu = v_ref[...] * _sigmoid(g_ref[...])
    _fill_history(scr, first, vh_ref[...] * _sigmoid(gh_ref[...]), glu, CF_HALO)
    n_sh = sh_scr.shape[1]
    for r in range(1, SUBLANE):
        sh_scr[r - 1] = scr[SUBLANE - r:SUBLANE - r + n_sh, :]
    kw = w_ref.shape[0]
    s1 = jnp.zeros((ts, LANE), F32)
    for c0 in range(0, c, LANE):
        acc = jnp.zeros((ts, LANE), F32) + cb_ref[:, c0:c0 + LANE]
        for d in range(kw):
            a, r = divmod(d, SUBLANE)
            if r == 0:
                tap = scr[CF_HALO - d:CF_HALO - d + ts, c0:c0 + LANE]
            else:
                off = CF_HALO - SUBLANE * (a + 1)
                tap = sh_scr[r - 1, off:off + ts, c0:c0 + LANE]
            acc = acc + w_ref[kw - 1 - d:kw - d, c0:c0 + LANE] * tap
        cf_scr[:, c0:c0 + LANE] = acc
        s1 = s1 + acc
    mu = jnp.sum(s1, axis=-1, keepdims=True) * (1.0 / c)
    s2 = jnp.zeros((ts, LANE), F32)
    for c0 in range(0, c, LANE):
        xc = cf_scr[:, c0:c0 + LANE] - mu
        s2 = s2 + xc * xc
    rstd = lax.rsqrt(jnp.sum(s2, axis=-1, keepdims=True) * (1.0 / c) + NORM_EPS)
    for c0 in range(0, c, LANE):
        y = (cf_scr[:, c0:c0 + LANE] - mu) * rstd * lw_ref[:, c0:c0 + LANE] + lb_ref[:, c0:c0 + LANE]
        o_ref[:, c0:c0 + LANE] = _silu(y).astype(o_ref.dtype)


def mix_conformer(sv, sg, w, cb, lw, lb, seq, ts=128):
    m = sv[0].shape[0]
    c = w.shape[1]
    kern = functools.partial(_mix_conformer_kernel, tiles_per_seq=seq // ts)
    vec = pl.BlockSpec((1, c), lambda i: (0, 0))
    return pl.pallas_call(
        kern,
        grid=(m // ts,),
        in_specs=[_main_spec(ts, c, sv[1]), _main_spec(ts, c, sg[1]),
                  _halo_spec(ts, CF_HALO, c, sv[1]), _halo_spec(ts, CF_HALO, c, sg[1]),
                  pl.BlockSpec(w.shape, lambda i: (0, 0)), vec, vec, vec],
        out_specs=pl.BlockSpec((ts, c), lambda i: (i, 0)),
        out_shape=jax.ShapeDtypeStruct((m, c), BF16),
        scratch_shapes=[pltpu.VMEM((CF_HALO + ts, c), F32),
                        pltpu.VMEM((SUBLANE - 1, CF_HALO - SUBLANE + ts, c), F32),
                        pltpu.VMEM((ts, c), F32)],
        compiler_params=_cparams(("parallel",)),
        name="mix_conformer",
    )(sv[0], sg[0], sv[0], sg[0], w, cb.reshape(1, c), lw.reshape(1, c), lb.reshape(1, c))


POOL_HALO = 16


def _mix_pool_kernel(u_ref, uh_ref, pw_ref, ps_ref, o_ref, scr, *, tiles_per_seq):
    ts, c = o_ref.shape
    ng = pw_ref.shape[0]
    cg = c // ng
    tile = pl.program_id(0) % tiles_per_seq
    first = tile == 0
    _fill_history(scr, first, uh_ref[...], u_ref[...], POOL_HALO)
    pos = tile * ts + lax.broadcasted_iota(jnp.int32, (ts, 1), 0)
    for gi, win in enumerate(POOL_WINDOWS):
        c0 = gi * cg
        cur = scr[POOL_HALO:POOL_HALO + ts, c0:c0 + cg]
        acc = cur
        for d in range(1, win):
            acc = acc + scr[POOL_HALO - d:POOL_HALO - d + ts, c0:c0 + cg]
        cnt = jnp.minimum(pos + 1, win).astype(F32)
        p = acc / cnt - cur
        y = _dot_bf16(p, pw_ref[gi])
        o_ref[:, c0:c0 + cg] = (y * ps_ref[:, c0:c0 + cg]).astype(o_ref.dtype)


def mix_pool(su, pool_w, pool_scale, seq, ts=256):
    m = su[0].shape[0]
    ng, cg, _ = pool_w.shape
    c = ng * cg
    kern = functools.partial(_mix_pool_kernel, tiles_per_seq=seq // ts)
    return pl.pallas_call(
        kern,
        grid=(m // ts,),
        in_specs=[_main_spec(ts, c, su[1]), _halo_spec(ts, POOL_HALO, c, su[1]),
                  pl.BlockSpec(pool_w.shape, lambda i: (0, 0, 0)),
                  pl.BlockSpec((1, c), lambda i: (0, 0))],
        out_specs=pl.BlockSpec((ts, c), lambda i: (i, 0)),
        out_shape=jax.ShapeDtypeStruct((m, c), BF16),
        scratch_shapes=[pltpu.VMEM((POOL_HALO + ts, c), F32)],
        compiler_params=_cparams(("parallel",)),
        name="mix_pool",
    )(su[0], su[0], pool_w.astype(BF16), pool_scale.reshape(1, c))


def _split3(x):
    x1 = x.astype(BF16)
    r1 = x - x1.astype(F32)
    x2 = r1.astype(BF16)
    x3 = (r1 - x2.astype(F32)).astype(BF16)
    return x1, x2, x3


P1_GROUP = 16


def _gdn_kernel(q_ref, k_ref, v_ref, qh_ref, kh_ref, vh_ref, z_ref, ab_ref,
                wq_ref, wk_ref, wv_ref, alog_ref, dtb_ref, nw_ref, o_ref,
                st_ref, hist, qs, ks, vs, gc_s, gcr_s, kn_s, kbq_s, vk_s, wqd_s, kd_s, u_s, a_s,
                *, n_heads):
    ts = o_ref.shape[0]
    dh = GDN_HEAD_DIM
    cc = GDN_CHUNK
    n_chunks = ts // cc
    first = pl.program_id(1) == 0

    @pl.when(first)
    def _():
        st_ref[...] = jnp.zeros_like(st_ref)

    def conv_silu(x_ref, xh_ref, w_ref, dst):
        _fill_history(hist, first, xh_ref[...], x_ref[...], SUBLANE)
        kw = w_ref.shape[0]
        acc = w_ref[kw - 1:kw, :] * x_ref[...]
        for d in range(1, kw):
            acc = acc + w_ref[kw - 1 - d:kw - d, :] * hist[SUBLANE - d:SUBLANE - d + ts, :]
        dst[...] = _silu(acc)

    conv_silu(q_ref, qh_ref, wq_ref, qs)
    conv_silu(k_ref, kh_ref, wk_ref, ks)
    conv_silu(v_ref, vh_ref, wv_ref, vs)

    ab = ab_ref[...]
    xg = ab + dtb_ref[...]
    softplus = jnp.maximum(xg, 0.0) + jnp.log1p(jnp.exp(-jnp.abs(xg)))
    g_all = -jnp.exp(alog_ref[...]) * softplus
    beta_all = _sigmoid(ab)

    rt = lax.broadcasted_iota(jnp.int32, (ts, ts), 0)
    ct = lax.broadcasted_iota(jnp.int32, (ts, ts), 1)
    tri_blk = jnp.where((rt >= ct) & ((rt // cc) == (ct // cc)), 1.0, 0.0).astype(BF16)
    g1, g2, g3 = _split3(g_all)
    gc_all = (jnp.dot(tri_blk, g1, preferred_element_type=F32)
              + jnp.dot(tri_blk, g2, preferred_element_type=F32)
              + jnp.dot(tri_blk, g3, preferred_element_type=F32))
    gc_s[...] = gc_all
    gcr_s[...] = gc_all.T

    scale = dh ** -0.5

    for h in range(n_heads):
        lanes = slice(h * dh, (h + 1) * dh)
        gc_col = jnp.broadcast_to(gc_all[:, h:h + 1], (ts, dh))
        b_col = jnp.broadcast_to(beta_all[:, n_heads + h:n_heads + h + 1], (ts, dh))
        q_all = qs[:, lanes]
        k_all = ks[:, lanes]
        q_all = q_all * (lax.rsqrt(jnp.sum(q_all * q_all, axis=-1, keepdims=True) + NORM_EPS) * scale)
        k_all = k_all * lax.rsqrt(jnp.sum(k_all * k_all, axis=-1, keepdims=True) + NORM_EPS)
        egc_all = jnp.exp(gc_col)
        kb_all = k_all * b_col
        kn_s[:, lanes] = k_all.astype(BF16)
        vk_s[:, 2 * h * dh:(2 * h + 1) * dh] = (vs[:, lanes] * b_col).astype(BF16)
        vk_s[:, (2 * h + 1) * dh:(2 * h + 2) * dh] = (kb_all * egc_all).astype(BF16)
        qd_all = q_all * egc_all
        for c in range(n_chunks):
            rows = slice(c * cc, (c + 1) * cc)
            gc_last = gc_col[(c + 1) * cc - 1:(c + 1) * cc, :]
            kbq_s[2 * c * cc:(2 * c + 1) * cc, lanes] = kb_all[rows].astype(BF16)
            kbq_s[(2 * c + 1) * cc:(2 * c + 2) * cc, lanes] = q_all[rows].astype(BF16)
            wqd_s[(2 * c + 1) * cc:(2 * c + 2) * cc, lanes] = qd_all[rows].astype(BF16)
            kd_s[rows, lanes] = (k_all[rows] * jnp.exp(gc_last - gc_col[rows])).astype(BF16)

    row = lax.broadcasted_iota(jnp.int32, (cc, cc), 0)
    col = lax.broadcasted_iota(jnp.int32, (cc, cc), 1)
    eye = (row == col).astype(F32)

    def p1_stages(c, h):
        rows = slice(c * cc, (c + 1) * cc)
        lanes = slice(h * dh, (h + 1) * dh)
        s = {}

        def s0():
            s["kq"] = lax.dot_general(kbq_s[2 * c * cc:(2 * c + 2) * cc, lanes], kn_s[rows, lanes], _NT,
                                      preferred_element_type=F32)

        def s1():
            pdiff = jnp.broadcast_to(gc_s[rows, h:h + 1], (cc, cc)) - gcr_s[h:h + 1, rows]
            decay = jnp.exp(jnp.where(row >= col, pdiff, -jnp.inf))
            kq = s.pop("kq")
            s["m"] = jnp.where(row > col, kq[:cc] * decay, 0.0)
            a_s[rows, h * cc:(h + 1) * cc] = (kq[cc:] * decay).astype(BF16)
            s["m1"] = s["m"].astype(BF16)
            s["p2"] = _dot_bf16(s["m1"], s["m1"])

        def s2():
            s["p2b"] = s["p2"].astype(BF16)
            s["p4"] = _dot_bf16(s["p2b"], s["p2b"])
            s["mp2"] = _dot_bf16(s.pop("m1"), s["p2b"])

        def s3():
            s["p4b"] = s["p4"].astype(BF16)
            s["p8"] = _dot_bf16(s["p4b"], s["p4b"])
            s["x1"] = eye - s.pop("m") + s.pop("p2") - s.pop("mp2")
            s.pop("p2b")

        def s4():
            p8b = s["p8"].astype(BF16)
            s["p16"] = _dot_bf16(p8b, p8b)
            s["p12"] = _dot_bf16(s.pop("p4b"), p8b)

        def s5():
            s["p16b"] = s["p16"].astype(BF16)
            s["p32"] = _dot_bf16(s["p16b"], s["p16b"])
            x2 = eye + s.pop("p4") + s.pop("p8") + s.pop("p12")
            s["x12"] = _dot_bf16(s.pop("x1"), x2)

        def s6():
            s["p48"] = _dot_bf16(s.pop("p16b"), s["p32"])

        def s7():
            x3 = eye + s.pop("p16") + s.pop("p32") + s.pop("p48")
            s["t"] = _dot_bf16(s.pop("x12"), x3)

        def s8():
            uw = _dot_bf16(s.pop("t"), vk_s[rows, 2 * h * dh:(2 * h + 2) * dh])
            u_s[rows, lanes] = uw[:, :dh]
            wqd_s[2 * c * cc:(2 * c + 1) * cc, lanes] = uw[:, dh:].astype(BF16)

        return (s0, s1, s2, s3, s4, s5, s6, s7, s8)

    chains = [(c, h) for c in range(n_chunks) for h in range(n_heads)]
    for g0 in range(0, len(chains), P1_GROUP):
        group = [p1_stages(c, h) for c, h in chains[g0:g0 + P1_GROUP]]
        for stage in zip(*group):
            for fn in stage:
                fn()

    for c in range(n_chunks):
        rows = slice(c * cc, (c + 1) * cc)
        states = [st_ref[h] for h in range(n_heads)]
        ws_qs = [lax.dot_general(wqd_s[2 * c * cc:(2 * c + 2) * cc, h * dh:(h + 1) * dh],
                                 states[h].astype(BF16), _NN, preferred_element_type=F32)
                 for h in range(n_heads)]
        v_new = [(u_s[rows, h * dh:(h + 1) * dh] - ws_qs[h][:cc]).astype(BF16) for h in range(n_heads)]
        o_intra = [lax.dot_general(a_s[rows, h * cc:(h + 1) * cc], v_new[h], _NN, preferred_element_type=F32)
                   for h in range(n_heads)]
        s_upd = [lax.dot_general(kd_s[rows, h * dh:(h + 1) * dh], v_new[h], _TN, preferred_element_type=F32)
                 for h in range(n_heads)]
        for h in range(n_heads):
            lanes = slice(h * dh, (h + 1) * dh)
            g_last = jnp.exp(jnp.broadcast_to(gc_s[(c + 1) * cc - 1:(c + 1) * cc, h:h + 1], (1, dh)))
            st_ref[h] = states[h] * g_last + s_upd[h]
            o = ws_qs[h][cc:] + o_intra[h]
            o = o * lax.rsqrt(jnp.mean(o * o, axis=-1, keepdims=True) + NORM_EPS) * nw_ref[...]
            o_ref[rows, lanes] = (o * _silu(z_ref[rows, lanes])).astype(o_ref.dtype)


def mix_gdn(sq, sk, sv, sz, sab, conv_w, a_log, dt_bias, norm_w, batch, seq, ts=256):
    m = sq[0].shape[0]
    c = conv_w.shape[1] // 3
    n_heads = c // GDN_HEAD_DIM
    nt = seq // ts
    r = ts // SUBLANE

    def main(colblk):
        return pl.BlockSpec((ts, c), lambda b, s: (b * nt + s, colblk))

    def halo(colblk):
        return pl.BlockSpec((SUBLANE, c), lambda b, s: (jnp.maximum((b * nt + s) * r - 1, 0), colblk))

    def wspec(j):
        return pl.BlockSpec((conv_w.shape[0], c), lambda b, s: (0, j))

    vec = pl.BlockSpec((1, LANE), lambda b, s: (0, 0))
    pad = LANE - n_heads
    alog = jnp.pad(a_log, (0, pad)).reshape(1, LANE)
    dtb = jnp.pad(dt_bias, (0, pad)).reshape(1, LANE)
    kern = functools.partial(_gdn_kernel, n_heads=n_heads)
    return pl.pallas_call(
        kern,
        grid=(batch, nt),
        in_specs=[main(sq[1]), main(sk[1]), main(sv[1]),
                  halo(sq[1]), halo(sk[1]), halo(sv[1]),
                  main(sz[1]),
                  pl.BlockSpec((ts, LANE), lambda b, s: (b * nt + s, sab[1])),
                  wspec(0), wspec(1), wspec(2), vec, vec, vec],
        out_specs=pl.BlockSpec((ts, c), lambda b, s: (b * nt + s, 0)),
        out_shape=jax.ShapeDtypeStruct((m, c), BF16),
        scratch_shapes=[pltpu.VMEM((n_heads, GDN_HEAD_DIM, GDN_HEAD_DIM), F32),
                        pltpu.VMEM((SUBLANE + ts, c), F32),
                        pltpu.VMEM((ts, c), F32),
                        pltpu.VMEM((ts, c), F32),
                        pltpu.VMEM((ts, c), F32),
                        pltpu.VMEM((ts, LANE), F32),
                        pltpu.VMEM((LANE, ts), F32),
                        pltpu.VMEM((ts, c), BF16),
                        pltpu.VMEM((2 * ts, c), BF16),
                        pltpu.VMEM((ts, 2 * c), BF16),
                        pltpu.VMEM((2 * ts, c), BF16),
                        pltpu.VMEM((ts, c), BF16),
                        pltpu.VMEM((ts, c), F32),
                        pltpu.VMEM((ts, n_heads * GDN_CHUNK), BF16)],
        compiler_params=_cparams(("arbitrary", "arbitrary")),
        name="mix_gdn",
    )(sq[0], sk[0], sv[0], sq[0], sk[0], sv[0], sz[0], sab[0],
      conv_w, conv_w, conv_w, alog, dtb, norm_w.reshape(1, LANE))


def _layer(x2, xb, ss, batch, seq, li, p, stacked, last):
    cg = p["sc_conv_w"].shape[1]
    n_heads = cg // GDN_HEAD_DIM
    n_main = 9 * cg
    n_ab = 2 * n_heads

    proj = matmul_in_proj(xb, ss, p["attn_norm_w"], stacked["w_in_t"], li, n_main, n_ab, cg, tm=IN_ROWS)

    y_a = mix_sconv((proj, 0), (proj, 1), (proj, 2), p["sc_conv_w"], seq)
    y_b = mix_conformer((proj, 3), (proj, 4), p["cf_conv_w"], p["cf_conv_b"], p["cf_ln_w"], p["cf_ln_b"], seq)
    y_c = mix_gdn((proj, 5), (proj, 6), (proj, 7), (proj, 8), (proj, (n_main + cg) // LANE),
                  p["gdn_conv_w"], p["gdn_a_log"], p["gdn_dt_bias"], p["gdn_norm_w"], batch, seq)
    y_d = mix_pool((proj, 9), p["pool_w"], p["pool_scale"], seq)

    x2, xb, ss = matmul4_residual((y_a, y_b, y_c, y_d), stacked["w_out"], li, x2, tm=OUT_ROWS, tn=OUT_TILE)

    wd = cast_weight(stacked["w_down"], li)
    act = matmul_swiglu(xb, ss, p["ffn_norm_w"], stacked["w_gate"], stacked["w_up"], li,
                        tm=FF_ROWS, tn=FF_TILE)
    if last:
        return matmul_residual(act, wd, x2, tm=DOWN_ROWS, tn=DOWN_TILE, emit_norm_inputs=False), None, None
    return matmul_residual(act, wd, x2, tm=DOWN_ROWS, tn=DOWN_TILE, emit_norm_inputs=True)


def kernel(x, attn_norm_w, w_in, sc_conv_w, cf_conv_w, cf_conv_b, cf_ln_w, cf_ln_b, gdn_conv_w,
           gdn_a_log, gdn_dt_bias, gdn_norm_w, pool_w, pool_scale, w_out, ffn_norm_w, w_gate, w_up,
           w_down, final_norm_w):
    batch, seq, d = x.shape
    small = dict(attn_norm_w=attn_norm_w, sc_conv_w=sc_conv_w, cf_conv_w=cf_conv_w,
                 cf_conv_b=cf_conv_b, cf_ln_w=cf_ln_w, cf_ln_b=cf_ln_b, gdn_conv_w=gdn_conv_w,
                 gdn_a_log=gdn_a_log, gdn_dt_bias=gdn_dt_bias, gdn_norm_w=gdn_norm_w, pool_w=pool_w,
                 pool_scale=pool_scale, ffn_norm_w=ffn_norm_w)
    stacked = dict(w_in_t=jnp.swapaxes(w_in, 1, 2), w_out=w_out, w_gate=w_gate, w_up=w_up, w_down=w_down)
    x2 = x.reshape(batch * seq, d)
    xb, ss = norm_inputs(x2)
    depth = attn_norm_w.shape[0]
    for li in range(depth):
        x2, xb, ss = _layer(x2, xb, ss, batch, seq, li, {name: val[li] for name, val in small.items()},
                            stacked, last=li == depth - 1)
    out = rmsnorm(x2, final_norm_w, x.dtype)
    return out.reshape(batch, seq, d)
```

```python
import functools

import jax
import jax.numpy as jnp
from jax import lax
from jax.experimental import pallas as pl
from jax.experimental.pallas import tpu as pltpu

NORM_EPS = 1e-6
GDN_HEAD_DIM = 128
GDN_CHUNK = 64
POOL_WINDOWS = (2, 4, 8, 16)
LANE = 128
SUBLANE = 8
VMEM_LIMIT = 56 * 1024 * 1024

IN_ROWS, IN_TILE = 2048, 512
OUT_ROWS, OUT_TILE = 1024, 512
FF_ROWS, FF_TILE = 1024, 512
DOWN_ROWS, DOWN_TILE, DOWN_K_SPLITS = 1024, 512, 2

BF16 = jnp.bfloat16
F32 = jnp.float32


def _cparams(sem):
    return pltpu.CompilerParams(dimension_semantics=sem, vmem_limit_bytes=VMEM_LIMIT)


def _sigmoid(x):
    return 1.0 / (1.0 + jnp.exp(-x))


def _silu(x):
    return x * _sigmoid(x)


_NN = (((1,), (0,)), ((), ()))
_NT = (((1,), (1,)), ((), ()))
_TN = (((0,), (0,)), ((), ()))


def _dot_bf16(a, b, dims=_NN):
    return lax.dot_general(a.astype(BF16), b.astype(BF16), dims, preferred_element_type=F32)


def _cast_kernel(x_ref, o_ref):
    o_ref[...] = x_ref[...].astype(o_ref.dtype)


def cast_weight(w_stacked, layer, tk=128):
    _, k, n = w_stacked.shape
    return pl.pallas_call(
        _cast_kernel,
        grid=(k // tk,),
        in_specs=[pl.BlockSpec((None, tk, n), lambda r: (layer, r, 0))],
        out_specs=pl.BlockSpec((tk, n), lambda r: (r, 0)),
        out_shape=jax.ShapeDtypeStruct((k, n), BF16),
        compiler_params=_cparams(("parallel",)),
        name="cast_weight",
    )(w_stacked)


def _rmsnorm_kernel(x_ref, w_ref, o_ref):
    x = x_ref[...]
    ms = jnp.mean(x * x, axis=-1, keepdims=True)
    o_ref[...] = (x * lax.rsqrt(ms + NORM_EPS) * w_ref[...]).astype(o_ref.dtype)


def rmsnorm(x, w, out_dtype, tm=256):
    m, d = x.shape
    return pl.pallas_call(
        _rmsnorm_kernel,
        grid=(m // tm,),
        in_specs=[pl.BlockSpec((tm, d), lambda i: (i, 0)),
                  pl.BlockSpec((1, d), lambda i: (0, 0))],
        out_specs=pl.BlockSpec((tm, d), lambda i: (i, 0)),
        out_shape=jax.ShapeDtypeStruct((m, d), out_dtype),
        compiler_params=_cparams(("parallel",)),
        name="rmsnorm",
    )(x, w.reshape(1, d))


def _panel_spec(tm, k):
    return pl.BlockSpec((tm, k), lambda i, j: (i, 0), pipeline_mode=pl.Buffered(1))


def _norm_inputs_kernel(x_ref, xb_ref, ss_ref):
    x = x_ref[...]
    xb_ref[...] = x.astype(xb_ref.dtype)
    ss_ref[...] = jnp.broadcast_to(jnp.sum(x * x, axis=-1, keepdims=True), ss_ref.shape)


def norm_inputs(x, tm=256):
    m, d = x.shape
    return pl.pallas_call(
        _norm_inputs_kernel,
        grid=(m // tm,),
        in_specs=[pl.BlockSpec((tm, d), lambda i: (i, 0))],
        out_specs=[pl.BlockSpec((tm, d), lambda i: (i, 0)), pl.BlockSpec((tm, LANE), lambda i: (i, 0))],
        out_shape=[jax.ShapeDtypeStruct((m, d), BF16), jax.ShapeDtypeStruct((m, LANE), F32)],
        compiler_params=_cparams(("parallel",)),
        name="norm_inputs",
    )(x)


def _row_scale(ss_ref, d):
    return lax.rsqrt(ss_ref[...] / d + NORM_EPS)


def _reset_row_sums(ss_scr):
    @pl.when(pl.program_id(1) == 0)
    def _():
        ss_scr[...] = jnp.zeros_like(ss_scr)


def _emit_norm_inputs(x_new, xb_ref, ss_ref, ss_scr):
    xb_ref[...] = x_new.astype(xb_ref.dtype)
    sq = x_new * x_new
    part = ss_scr[...]
    for c0 in range(0, sq.shape[1], LANE):
        part = part + sq[:, c0:c0 + LANE]
    ss_scr[...] = part
    ss_ref[...] = jnp.broadcast_to(jnp.sum(part, axis=-1, keepdims=True), ss_ref.shape)


def _tile_dispatch(tile, tn, last_width):
    if last_width == tn:
        tile(tn)
    else:
        is_last = pl.program_id(1) == pl.num_programs(1) - 1
        pl.when(jnp.logical_not(is_last))(lambda: tile(tn))
        pl.when(is_last)(lambda: tile(last_width))


def _mm_in_kernel(a_ref, ss_ref, nw_ref, bt_ref, o_ref, *, last_width):
    r = _row_scale(ss_ref, a_ref.shape[1])

    def tile(width):
        bt = (bt_ref[:width, :] * nw_ref[...]).astype(BF16)
        acc = lax.dot_general(a_ref[...], bt, _NT, preferred_element_type=F32)
        for c0 in range(0, width, LANE):
            o_ref[:, c0:c0 + LANE] = acc[:, c0:c0 + LANE] * r

    _tile_dispatch(tile, o_ref.shape[1], last_width)


def matmul_in_proj(xb, ss, norm_w, w_in_t, layer, n_main, n_ab, n_pool, tm):
    m, k = xb.shape
    _, n, _ = w_in_t.shape
    main_blocks = n_main // IN_TILE
    ab_block = main_blocks + n_pool // IN_TILE

    def src_row(j):
        return jnp.where(j < main_blocks, j * IN_TILE,
                         jnp.where(j < ab_block, n_main + n_ab + (j - main_blocks) * IN_TILE, n_main))

    assert n_main + IN_TILE <= n
    return pl.pallas_call(
        functools.partial(_mm_in_kernel, last_width=LANE),
        grid=(m // tm, ab_block + 1),
        in_specs=[_panel_spec(tm, k),
                  pl.BlockSpec((tm, LANE), lambda i, j: (i, 0)),
                  pl.BlockSpec((1, k), lambda i, j: (0, 0)),
                  pl.BlockSpec((pl.Element(IN_TILE), pl.Element(k)),
                               lambda i, j: (pl.multiple_of(layer * n + src_row(j), SUBLANE), 0))],
        out_specs=pl.BlockSpec((tm, IN_TILE), lambda i, j: (i, j)),
        out_shape=jax.ShapeDtypeStruct((m, ab_block * IN_TILE + LANE), F32),
        compiler_params=_cparams(("parallel", "arbitrary")),
        name="in_proj",
    )(xb, ss, norm_w.reshape(1, k), w_in_t.reshape(-1, k))


def _norm_out_specs(tm, tn):
    return [pl.BlockSpec((tm, tn), lambda i, j: (i, j)),
            pl.BlockSpec((tm, tn), lambda i, j: (i, j)),
            pl.BlockSpec((tm, LANE), lambda i, j: (i, 0))]


def _norm_out_shapes(m, n):
    return [jax.ShapeDtypeStruct((m, n), F32), jax.ShapeDtypeStruct((m, n), BF16),
            jax.ShapeDtypeStruct((m, LANE), F32)]


def _mm_res_kernel(a_ref, b_ref, r_ref, o_ref, *norm_refs):
    if norm_refs:
        _reset_row_sums(norm_refs[-1])
    x_new = r_ref[...] + jnp.dot(a_ref[...], b_ref[...], preferred_element_type=F32)
    o_ref[...] = x_new
    if norm_refs:
        _emit_norm_inputs(x_new, *norm_refs)


def matmul_residual(a, b, res, tm, tn, k_splits, emit_norm_inputs):
    m, k = a.shape
    n = b.shape[1]
    kb = k // k_splits
    assert kb * k_splits == k and kb % LANE == 0
    x_spec = pl.BlockSpec((tm, tn), lambda i, j: (i, j))
    x = res
    for s in range(k_splits):
        emit = emit_norm_inputs and s == k_splits - 1
        x = pl.pallas_call(
            _mm_res_kernel,
            grid=(m // tm, n // tn),
            in_specs=[pl.BlockSpec((tm, kb), lambda i, j, s=s: (i, s)),
                      pl.BlockSpec((kb, tn), lambda i, j, s=s: (s, j)),
                      x_spec],
            out_specs=_norm_out_specs(tm, tn) if emit else x_spec,
            out_shape=_norm_out_shapes(m, n) if emit else jax.ShapeDtypeStruct((m, n), F32),
            scratch_shapes=[pltpu.VMEM((tm, LANE), F32)] if emit else [],
            compiler_params=_cparams(("parallel", "arbitrary")),
            name="matmul_residual",
        )(a, b, x)
    return x


def _layer_tile_spec(k, tn, layer):
    return pl.BlockSpec((None, k, tn), lambda i, j: (layer, 0, j))


def _mm4_res_kernel(a0, a1, a2, a3, b_ref, r_ref, o_ref, xb_ref, ss_ref, ss_scr):
    kg = a0.shape[1]
    _reset_row_sums(ss_scr)
    acc = r_ref[...]
    for g, a in enumerate((a0, a1, a2, a3)):
        acc = acc + jnp.dot(a[...], b_ref[g * kg:(g + 1) * kg, :], preferred_element_type=F32)
    o_ref[...] = acc
    _emit_norm_inputs(acc, xb_ref, ss_ref, ss_scr)


def matmul4_residual(parts, b, res, tm, tn):
    m, kg = parts[0].shape
    k, n = b.shape
    a_spec = pl.BlockSpec((tm, kg), lambda i, j: (i, 0))
    return pl.pallas_call(
        _mm4_res_kernel,
        grid=(m // tm, n // tn),
        in_specs=[a_spec, a_spec, a_spec, a_spec,
                  pl.BlockSpec((k, tn), lambda i, j: (0, j)),
                  pl.BlockSpec((tm, tn), lambda i, j: (i, j))],
        out_specs=_norm_out_specs(tm, tn),
        out_shape=_norm_out_shapes(m, n),
        scratch_shapes=[pltpu.VMEM((tm, LANE), F32)],
        compiler_params=_cparams(("parallel", "arbitrary")),
        name="out_proj",
    )(*parts, b, res)


def _mm_glu_kernel(a_ref, ss_ref, nw_ref, bg_ref, bu_ref, o_ref, *, last_width):
    r = _row_scale(ss_ref, a_ref.shape[1])

    def scaled(b_ref, width):
        return jnp.concatenate([b_ref[:, c0:c0 + LANE] * nw_ref[...] for c0 in range(0, width, LANE)],
                               axis=1).astype(BF16)

    def tile(width):
        a = a_ref[...]
        g = jnp.dot(a, scaled(bg_ref, width), preferred_element_type=F32)
        u = jnp.dot(a, scaled(bu_ref, width), preferred_element_type=F32)
        for c0 in range(0, width, LANE):
            gate = g[:, c0:c0 + LANE] * r
            o_ref[:, c0:c0 + LANE] = (_silu(gate) * (u[:, c0:c0 + LANE] * r)).astype(o_ref.dtype)

    _tile_dispatch(tile, o_ref.shape[1], last_width)


def matmul_swiglu(xb, ss, norm_w, wg_stacked, wu_stacked, layer, tm, tn):
    m, k = xb.shape
    n = wg_stacked.shape[2]
    b_spec = _layer_tile_spec(k, tn, layer)
    last_width = n - (pl.cdiv(n, tn) - 1) * tn
    return pl.pallas_call(
        functools.partial(_mm_glu_kernel, last_width=last_width),
        grid=(m // tm, pl.cdiv(n, tn)),
        in_specs=[_panel_spec(tm, k),
                  pl.BlockSpec((tm, LANE), lambda i, j: (i, 0)),
                  pl.BlockSpec((k, LANE), lambda i, j: (0, 0)),
                  b_spec, b_spec],
        out_specs=pl.BlockSpec((tm, tn), lambda i, j: (i, j)),
        out_shape=jax.ShapeDtypeStruct((m, n), BF16),
        compiler_params=_cparams(("parallel", "arbitrary")),
        name="ffn_gate_up",
    )(xb, ss, jnp.broadcast_to(norm_w.reshape(k, 1), (k, LANE)), wg_stacked, wu_stacked)


def _halo_spec(ts, hb, c, col):
    r = ts // hb
    return pl.BlockSpec((hb, c), lambda i: (jnp.maximum(i * r - 1, 0), col))


def _main_spec(ts, c, col):
    return pl.BlockSpec((ts, c), lambda i: (i, col))


def _fill_history(scr, first, halo, cur, hb):
    scr[0:hb, :] = jnp.where(first, 0.0, halo)
    scr[hb:, :] = cur


def _mix_sconv_kernel(b_ref, c_ref, h_ref, ch_ref, hh_ref, w_ref, o_ref, scr, *, tiles_per_seq):
    ts = o_ref.shape[0]
    first = (pl.program_id(0) % tiles_per_seq) == 0
    m = c_ref[...] * h_ref[...]
    _fill_history(scr, first, ch_ref[...] * hh_ref[...], m, SUBLANE)
    w = w_ref[...]
    kw = w.shape[0]
    y = w[kw - 1:kw, :] * m
    for d in range(1, kw):
        y = y + w[kw - 1 - d:kw - d, :] * scr[SUBLANE - d:SUBLANE - d + ts, :]
    o_ref[...] = (b_ref[...] * y).astype(o_ref.dtype)


def mix_sconv(sb, sc, sh, w, seq, ts=256):
    m = sb[0].shape[0]
    c = w.shape[1]
    kern = functools.partial(_mix_sconv_kernel, tiles_per_seq=seq // ts)
    return pl.pallas_call(
        kern,
        grid=(m // ts,),
        in_specs=[_main_spec(ts, c, sb[1]), _main_spec(ts, c, sc[1]), _main_spec(ts, c, sh[1]),
                  _halo_spec(ts, SUBLANE, c, sc[1]), _halo_spec(ts, SUBLANE, c, sh[1]),
                  pl.BlockSpec(w.shape, lambda i: (0, 0))],
        out_specs=pl.BlockSpec((ts, c), lambda i: (i, 0)),
        out_shape=jax.ShapeDtypeStruct((m, c), BF16),
        scratch_shapes=[pltpu.VMEM((SUBLANE + ts, c), F32)],
        compiler_params=_cparams(("parallel",)),
        name="mix_sconv",
    )(sb[0], sc[0], sh[0], sc[0], sh[0], w)


CF_HALO = 32


def _mix_conformer_kernel(v_ref, g_ref, vh_ref, gh_ref, w_ref, cb_ref, lw_ref, lb_ref, o_ref,
                          scr, sh_scr, cf_scr, *, tiles_per_seq):
    ts, c = o_ref.shape
    first = (pl.program_id(0) % tiles_per_seq) == 0
    glu = v_ref[...] * _sigmoid(g_ref[...])
    _fill_history(scr, first, vh_ref[...] * _sigmoid(gh_ref[...]), glu, CF_HALO)
    n_sh = sh_scr.shape[1]
    for r in range(1, SUBLANE):
        sh_scr[r - 1] = scr[SUBLANE - r:SUBLANE - r + n_sh, :]
    kw = w_ref.shape[0]
    s1 = jnp.zeros((ts, LANE), F32)
    for c0 in range(0, c, LANE):
        acc = jnp.zeros((ts, LANE), F32) + cb_ref[:, c0:c0 + LANE]
        for d in range(kw):
            a, r = divmod(d, SUBLANE)
            if r == 0:
                tap = scr[CF_HALO - d:CF_HALO - d + ts, c0:c0 + LANE]
            else:
                off = CF_HALO - SUBLANE * (a + 1)
                tap = sh_scr[r - 1, off:off + ts, c0:c0 + LANE]
            acc = acc + w_ref[kw - 1 - d:kw - d, c0:c0 + LANE] * tap
        cf_scr[:, c0:c0 + LANE] = acc
        s1 = s1 + acc
    mu = jnp.sum(s1, axis=-1, keepdims=True) * (1.0 / c)
    s2 = jnp.zeros((ts, LANE), F32)
    for c0 in range(0, c, LANE):
        xc = cf_scr[:, c0:c0 + LANE] - mu
        s2 = s2 + xc * xc
    rstd = lax.rsqrt(jnp.sum(s2, axis=-1, keepdims=True) * (1.0 / c) + NORM_EPS)
    for c0 in range(0, c, LANE):
        y = (cf_scr[:, c0:c0 + LANE] - mu) * rstd * lw_ref[:, c0:c0 + LANE] + lb_ref[:, c0:c0 + LANE]
        o_ref[:, c0:c0 + LANE] = _silu(y).astype(o_ref.dtype)


def mix_conformer(sv, sg, w, cb, lw, lb, seq, ts=128):
    m = sv[0].shape[0]
    c = w.shape[1]
    kern = functools.partial(_mix_conformer_kernel, tiles_per_seq=seq // ts)
    vec = pl.BlockSpec((1, c), lambda i: (0, 0))
    return pl.pallas_call(
        kern,
        grid=(m // ts,),
        in_specs=[_main_spec(ts, c, sv[1]), _main_spec(ts, c, sg[1]),
                  _halo_spec(ts, CF_HALO, c, sv[1]), _halo_spec(ts, CF_HALO, c, sg[1]),
                  pl.BlockSpec(w.shape, lambda i: (0, 0)), vec, vec, vec],
        out_specs=pl.BlockSpec((ts, c), lambda i: (i, 0)),
        out_shape=jax.ShapeDtypeStruct((m, c), BF16),
        scratch_shapes=[pltpu.VMEM((CF_HALO + ts, c), F32),
                        pltpu.VMEM((SUBLANE - 1, CF_HALO - SUBLANE + ts, c), F32),
                        pltpu.VMEM((ts, c), F32)],
        compiler_params=_cparams(("parallel",)),
        name="mix_conformer",
    )(sv[0], sg[0], sv[0], sg[0], w, cb.reshape(1, c), lw.reshape(1, c), lb.reshape(1, c))


POOL_HALO = 16


def _mix_pool_kernel(u_ref, uh_ref, pw_ref, ps_ref, o_ref, scr, *, tiles_per_seq):
    ts, c = o_ref.shape
    ng = pw_ref.shape[0]
    cg = c // ng
    tile = pl.program_id(0) % tiles_per_seq
    first = tile == 0
    _fill_history(scr, first, uh_ref[...], u_ref[...], POOL_HALO)
    pos = tile * ts + lax.broadcasted_iota(jnp.int32, (ts, 1), 0)
    for gi, win in enumerate(POOL_WINDOWS):
        c0 = gi * cg
        cur = scr[POOL_HALO:POOL_HALO + ts, c0:c0 + cg]
        acc = cur
        for d in range(1, win):
            acc = acc + scr[POOL_HALO - d:POOL_HALO - d + ts, c0:c0 + cg]
        cnt = jnp.minimum(pos + 1, win).astype(F32)
        p = acc / cnt - cur
        y = _dot_bf16(p, pw_ref[gi])
        o_ref[:, c0:c0 + cg] = (y * ps_ref[:, c0:c0 + cg]).astype(o_ref.dtype)


def mix_pool(su, pool_w, pool_scale, seq, ts=256):
    m = su[0].shape[0]
    ng, cg, _ = pool_w.shape
    c = ng * cg
    kern = functools.partial(_mix_pool_kernel, tiles_per_seq=seq // ts)
    return pl.pallas_call(
        kern,
        grid=(m // ts,),
        in_specs=[_main_spec(ts, c, su[1]), _halo_spec(ts, POOL_HALO, c, su[1]),
                  pl.BlockSpec(pool_w.shape, lambda i: (0, 0, 0)),
                  pl.BlockSpec((1, c), lambda i: (0, 0))],
        out_specs=pl.BlockSpec((ts, c), lambda i: (i, 0)),
        out_shape=jax.ShapeDtypeStruct((m, c), BF16),
        scratch_shapes=[pltpu.VMEM((POOL_HALO + ts, c), F32)],
        compiler_params=_cparams(("parallel",)),
        name="mix_pool",
    )(su[0], su[0], pool_w.astype(BF16), pool_scale.reshape(1, c))


def _split3(x):
    x1 = x.astype(BF16)
    r1 = x - x1.astype(F32)
    x2 = r1.astype(BF16)
    x3 = (r1 - x2.astype(F32)).astype(BF16)
    return x1, x2, x3


P1_GROUP = 16


def _gdn_kernel(q_ref, k_ref, v_ref, qh_ref, kh_ref, vh_ref, z_ref, ab_ref,
                wq_ref, wk_ref, wv_ref, alog_ref, dtb_ref, nw_ref, o_ref,
                st_ref, hist, qs, ks, vs, gc_s, gcr_s, kn_s, kbq_s, vk_s, wqd_s, kd_s, u_s, a_s,
                *, n_heads):
    ts = o_ref.shape[0]
    dh = GDN_HEAD_DIM
    cc = GDN_CHUNK
    n_chunks = ts // cc
    first = pl.program_id(1) == 0

    @pl.when(first)
    def _():
        st_ref[...] = jnp.zeros_like(st_ref)

    def conv_silu(x_ref, xh_ref, w_ref, dst):
        _fill_history(hist, first, xh_ref[...], x_ref[...], SUBLANE)
        kw = w_ref.shape[0]
        acc = w_ref[kw - 1:kw, :] * x_ref[...]
        for d in range(1, kw):
            acc = acc + w_ref[kw - 1 - d:kw - d, :] * hist[SUBLANE - d:SUBLANE - d + ts, :]
        dst[...] = _silu(acc)

    conv_silu(q_ref, qh_ref, wq_ref, qs)
    conv_silu(k_ref, kh_ref, wk_ref, ks)
    conv_silu(v_ref, vh_ref, wv_ref, vs)

    ab = ab_ref[...]
    xg = ab + dtb_ref[...]
    softplus = jnp.maximum(xg, 0.0) + jnp.log1p(jnp.exp(-jnp.abs(xg)))
    g_all = -jnp.exp(alog_ref[...]) * softplus
    beta_all = _sigmoid(ab)

    rt = lax.broadcasted_iota(jnp.int32, (ts, ts), 0)
    ct = lax.broadcasted_iota(jnp.int32, (ts, ts), 1)
    tri_blk = jnp.where((rt >= ct) & ((rt // cc) == (ct // cc)), 1.0, 0.0).astype(BF16)
    g1, g2, g3 = _split3(g_all)
    gc_all = (jnp.dot(tri_blk, g1, preferred_element_type=F32)
              + jnp.dot(tri_blk, g2, preferred_element_type=F32)
              + jnp.dot(tri_blk, g3, preferred_element_type=F32))
    gc_s[...] = gc_all
    gcr_s[...] = gc_all.T

    scale = dh ** -0.5

    for h in range(n_heads):
        lanes = slice(h * dh, (h + 1) * dh)
        gc_col = jnp.broadcast_to(gc_all[:, h:h + 1], (ts, dh))
        b_col = jnp.broadcast_to(beta_all[:, n_heads + h:n_heads + h + 1], (ts, dh))
        q_all = qs[:, lanes]
        k_all = ks[:, lanes]
        q_all = q_all * (lax.rsqrt(jnp.sum(q_all * q_all, axis=-1, keepdims=True) + NORM_EPS) * scale)
        k_all = k_all * lax.rsqrt(jnp.sum(k_all * k_all, axis=-1, keepdims=True) + NORM_EPS)
        egc_all = jnp.exp(gc_col)
        kb_all = k_all * b_col
        kn_s[:, lanes] = k_all.astype(BF16)
        vk_s[:, 2 * h * dh:(2 * h + 1) * dh] = (vs[:, lanes] * b_col).astype(BF16)
        vk_s[:, (2 * h + 1) * dh:(2 * h + 2) * dh] = (kb_all * egc_all).astype(BF16)
        qd_all = q_all * egc_all
        for c in range(n_chunks):
            rows = slice(c * cc, (c + 1) * cc)
            gc_last = gc_col[(c + 1) * cc - 1:(c + 1) * cc, :]
            kbq_s[2 * c * cc:(2 * c + 1) * cc, lanes] = kb_all[rows].astype(BF16)
            kbq_s[(2 * c + 1) * cc:(2 * c + 2) * cc, lanes] = q_all[rows].astype(BF16)
            wqd_s[(2 * c + 1) * cc:(2 * c + 2) * cc, lanes] = qd_all[rows].astype(BF16)
            kd_s[rows, lanes] = (k_all[rows] * jnp.exp(gc_last - gc_col[rows])).astype(BF16)

    row = lax.broadcasted_iota(jnp.int32, (cc, cc), 0)
    col = lax.broadcasted_iota(jnp.int32, (cc, cc), 1)
    eye = (row == col).astype(F32)

    def p1_stages(c, h):
        rows = slice(c * cc, (c + 1) * cc)
        lanes = slice(h * dh, (h + 1) * dh)
        s = {}

        def s0():
            s["kq"] = lax.dot_general(kbq_s[2 * c * cc:(2 * c + 2) * cc, lanes], kn_s[rows, lanes], _NT,
                                      preferred_element_type=F32)

        def s1():
            pdiff = jnp.broadcast_to(gc_s[rows, h:h + 1], (cc, cc)) - gcr_s[h:h + 1, rows]
            decay = jnp.exp(jnp.where(row >= col, pdiff, -jnp.inf))
            kq = s.pop("kq")
            s["m"] = jnp.where(row > col, kq[:cc] * decay, 0.0)
            a_s[rows, h * cc:(h + 1) * cc] = (kq[cc:] * decay).astype(BF16)
            s["m1"] = s["m"].astype(BF16)
            s["p2"] = _dot_bf16(s["m1"], s["m1"])

        def s2():
            s["p2b"] = s["p2"].astype(BF16)
            s["p4"] = _dot_bf16(s["p2b"], s["p2b"])
            s["mp2"] = _dot_bf16(s.pop("m1"), s["p2b"])

        def s3():
            s["p4b"] = s["p4"].astype(BF16)
            s["p8"] = _dot_bf16(s["p4b"], s["p4b"])
            s["x1"] = eye - s.pop("m") + s.pop("p2") - s.pop("mp2")
            s.pop("p2b")

        def s4():
            p8b = s["p8"].astype(BF16)
            s["p16"] = _dot_bf16(p8b, p8b)
            s["p12"] = _dot_bf16(s.pop("p4b"), p8b)

        def s5():
            s["p16b"] = s["p16"].astype(BF16)
            s["p32"] = _dot_bf16(s["p16b"], s["p16b"])
            x2 = eye + s.pop("p4") + s.pop("p8") + s.pop("p12")
            s["x12"] = _dot_bf16(s.pop("x1"), x2)

        def s6():
            s["p48"] = _dot_bf16(s.pop("p16b"), s["p32"])

        def s7():
            x3 = eye + s.pop("p16") + s.pop("p32") + s.pop("p48")
            s["t"] = _dot_bf16(s.pop("x12"), x3)

        def s8():
            uw = _dot_bf16(s.pop("t"), vk_s[rows, 2 * h * dh:(2 * h + 2) * dh])
            u_s[rows, lanes] = uw[:, :dh]
            wqd_s[2 * c * cc:(2 * c + 1) * cc, lanes] = uw[:, dh:].astype(BF16)

        return (s0, s1, s2, s3, s4, s5, s6, s7, s8)

    chains = [(c, h) for c in range(n_chunks) for h in range(n_heads)]
    for g0 in range(0, len(chains), P1_GROUP):
        group = [p1_stages(c, h) for c, h in chains[g0:g0 + P1_GROUP]]
        for stage in zip(*group):
            for fn in stage:
                fn()

    for c in range(n_chunks):
        rows = slice(c * cc, (c + 1) * cc)
        states = [st_ref[h] for h in range(n_heads)]
        ws_qs = [lax.dot_general(wqd_s[2 * c * cc:(2 * c + 2) * cc, h * dh:(h + 1) * dh],
                                 states[h].astype(BF16), _NN, preferred_element_type=F32)
                 for h in range(n_heads)]
        v_new = [(u_s[rows, h * dh:(h + 1) * dh] - ws_qs[h][:cc]).astype(BF16) for h in range(n_heads)]
        o_intra = [lax.dot_general(a_s[rows, h * cc:(h + 1) * cc], v_new[h], _NN, preferred_element_type=F32)
                   for h in range(n_heads)]
        s_upd = [lax.dot_general(kd_s[rows, h * dh:(h + 1) * dh], v_new[h], _TN, preferred_element_type=F32)
                 for h in range(n_heads)]
        for h in range(n_heads):
            lanes = slice(h * dh, (h + 1) * dh)
            g_last = jnp.exp(jnp.broadcast_to(gc_s[(c + 1) * cc - 1:(c + 1) * cc, h:h + 1], (1, dh)))
            st_ref[h] = states[h] * g_last + s_upd[h]
            o = ws_qs[h][cc:] + o_intra[h]
            o = o * lax.rsqrt(jnp.mean(o * o, axis=-1, keepdims=True) + NORM_EPS) * nw_ref[...]
            o_ref[rows, lanes] = (o * _silu(z_ref[rows, lanes])).astype(o_ref.dtype)


def mix_gdn(sq, sk, sv, sz, sab, conv_w, a_log, dt_bias, norm_w, batch, seq, ts=256):
    m = sq[0].shape[0]
    c = conv_w.shape[1] // 3
    n_heads = c // GDN_HEAD_DIM
    nt = seq // ts
    r = ts // SUBLANE

    def main(colblk):
        return pl.BlockSpec((ts, c), lambda b, s: (b * nt + s, colblk))

    def halo(colblk):
        return pl.BlockSpec((SUBLANE, c), lambda b, s: (jnp.maximum((b * nt + s) * r - 1, 0), colblk))

    def wspec(j):
        return pl.BlockSpec((conv_w.shape[0], c), lambda b, s: (0, j))

    vec = pl.BlockSpec((1, LANE), lambda b, s: (0, 0))
    pad = LANE - n_heads
    alog = jnp.pad(a_log, (0, pad)).reshape(1, LANE)
    dtb = jnp.pad(dt_bias, (0, pad)).reshape(1, LANE)
    kern = functools.partial(_gdn_kernel, n_heads=n_heads)
    return pl.pallas_call(
        kern,
        grid=(batch, nt),
        in_specs=[main(sq[1]), main(sk[1]), main(sv[1]),
                  halo(sq[1]), halo(sk[1]), halo(sv[1]),
                  main(sz[1]),
                  pl.BlockSpec((ts, LANE), lambda b, s: (b * nt + s, sab[1])),
                  wspec(0), wspec(1), wspec(2), vec, vec, vec],
        out_specs=pl.BlockSpec((ts, c), lambda b, s: (b * nt + s, 0)),
        out_shape=jax.ShapeDtypeStruct((m, c), BF16),
        scratch_shapes=[pltpu.VMEM((n_heads, GDN_HEAD_DIM, GDN_HEAD_DIM), F32),
                        pltpu.VMEM((SUBLANE + ts, c), F32),
                        pltpu.VMEM((ts, c), F32),
                        pltpu.VMEM((ts, c), F32),
                        pltpu.VMEM((ts, c), F32),
                        pltpu.VMEM((ts, LANE), F32),
                        pltpu.VMEM((LANE, ts), F32),
                        pltpu.VMEM((ts, c), BF16),
                        pltpu.VMEM((2 * ts, c), BF16),
                        pltpu.VMEM((ts, 2 * c), BF16),
                        pltpu.VMEM((2 * ts, c), BF16),
                        pltpu.VMEM((ts, c), BF16),
                        pltpu.VMEM((ts, c), F32),
                        pltpu.VMEM((ts, n_heads * GDN_CHUNK), BF16)],
        compiler_params=_cparams(("arbitrary", "arbitrary")),
        name="mix_gdn",
    )(sq[0], sk[0], sv[0], sq[0], sk[0], sv[0], sz[0], sab[0],
      conv_w, conv_w, conv_w, alog, dtb, norm_w.reshape(1, LANE))


def _layer(x2, xb, ss, batch, seq, li, p, stacked, last):
    cg = p["sc_conv_w"].shape[1]
    n_heads = cg // GDN_HEAD_DIM
    n_main = 9 * cg
    n_ab = 2 * n_heads

    proj = matmul_in_proj(xb, ss, p["attn_norm_w"], stacked["w_in_t"], li, n_main, n_ab, cg, tm=IN_ROWS)

    y_a = mix_sconv((proj, 0), (proj, 1), (proj, 2), p["sc_conv_w"], seq)
    y_b = mix_conformer((proj, 3), (proj, 4), p["cf_conv_w"], p["cf_conv_b"], p["cf_ln_w"], p["cf_ln_b"], seq)
    y_c = mix_gdn((proj, 5), (proj, 6), (proj, 7), (proj, 8), (proj, (n_main + cg) // LANE),
                  p["gdn_conv_w"], p["gdn_a_log"], p["gdn_dt_bias"], p["gdn_norm_w"], batch, seq)
    y_d = mix_pool((proj, 9), p["pool_w"], p["pool_scale"], seq)

    x2, xb, ss = matmul4_residual((y_a, y_b, y_c, y_d), cast_weight(stacked["w_out"], li), x2,
                                  tm=OUT_ROWS, tn=OUT_TILE)

    wd = cast_weight(stacked["w_down"], li)
    act = matmul_swiglu(xb, ss, p["ffn_norm_w"], stacked["w_gate"], stacked["w_up"], li,
                        tm=FF_ROWS, tn=FF_TILE)
    if last:
        return matmul_residual(act, wd, x2, DOWN_ROWS, DOWN_TILE, DOWN_K_SPLITS, emit_norm_inputs=False), None, None
    return matmul_residual(act, wd, x2, DOWN_ROWS, DOWN_TILE, DOWN_K_SPLITS, emit_norm_inputs=True)


def kernel(x, attn_norm_w, w_in, sc_conv_w, cf_conv_w, cf_conv_b, cf_ln_w, cf_ln_b, gdn_conv_w,
           gdn_a_log, gdn_dt_bias, gdn_norm_w, pool_w, pool_scale, w_out, ffn_norm_w, w_gate, w_up,
           w_down, final_norm_w):
    batch, seq, d = x.shape
    small = dict(attn_norm_w=attn_norm_w, sc_conv_w=sc_conv_w, cf_conv_w=cf_conv_w,
                 cf_conv_b=cf_conv_b, cf_ln_w=cf_ln_w, cf_ln_b=cf_ln_b, gdn_conv_w=gdn_conv_w,
                 gdn_a_log=gdn_a_log, gdn_dt_bias=gdn_dt_bias, gdn_norm_w=gdn_norm_w, pool_w=pool_w,
                 pool_scale=pool_scale, ffn_norm_w=ffn_norm_w)
    stacked = dict(w_in_t=jnp.swapaxes(w_in, 1, 2), w_out=w_out, w_gate=w_gate, w_up=w_up, w_down=w_down)
    x2 = x.reshape(batch * seq, d)
    xb, ss = norm_inputs(x2)
    depth = attn_norm_w.shape[0]
    for li in range(depth):
        x2, xb, ss = _layer(x2, xb, ss, batch, seq, li, {name: val[li] for name, val in small.items()},
                            stacked, last=li == depth - 1)
    out = rmsnorm(x2, final_norm_w, x.dtype)
    return out.reshape(batch, seq, d)
```

```python
import functools

import jax
import jax.numpy as jnp
from jax import lax
from jax.experimental import pallas as pl
from jax.experimental.pallas import tpu as pltpu

NORM_EPS = 1e-6
GDN_HEAD_DIM = 128
GDN_CHUNK = 64
POOL_WINDOWS = (2, 4, 8, 16)
LANE = 128
SUBLANE = 8
VMEM_LIMIT = 56 * 1024 * 1024

IN_ROWS, IN_TILE = 2048, 512
OUT_ROWS, OUT_TILE = 1024, 512
FF_ROWS, FF_TILE = 1024, 512
DOWN_ROWS, DOWN_TILE, DOWN_K_SPLITS = 1024, 512, 2

BF16 = jnp.bfloat16
F32 = jnp.float32


def _cparams(sem):
    return pltpu.CompilerParams(dimension_semantics=sem, vmem_limit_bytes=VMEM_LIMIT)


def _sigmoid(x):
    return 1.0 / (1.0 + jnp.exp(-x))


def _silu(x):
    return x * _sigmoid(x)


_NN = (((1,), (0,)), ((), ()))
_NT = (((1,), (1,)), ((), ()))
_TN = (((0,), (0,)), ((), ()))


def _dot_bf16(a, b, dims=_NN):
    return lax.dot_general(a.astype(BF16), b.astype(BF16), dims, preferred_element_type=F32)


def _cast_kernel(x_ref, o_ref):
    o_ref[...] = x_ref[...].astype(o_ref.dtype)


def cast_weight(w_stacked, layer, tk=128):
    _, k, n = w_stacked.shape
    return pl.pallas_call(
        _cast_kernel,
        grid=(k // tk,),
        in_specs=[pl.BlockSpec((None, tk, n), lambda r: (layer, r, 0))],
        out_specs=pl.BlockSpec((tk, n), lambda r: (r, 0)),
        out_shape=jax.ShapeDtypeStruct((k, n), BF16),
        compiler_params=_cparams(("parallel",)),
        name="cast_weight",
    )(w_stacked)


def _rmsnorm_kernel(x_ref, w_ref, o_ref):
    x = x_ref[...]
    ms = jnp.mean(x * x, axis=-1, keepdims=True)
    o_ref[...] = (x * lax.rsqrt(ms + NORM_EPS) * w_ref[...]).astype(o_ref.dtype)


def rmsnorm(x, w, out_dtype, tm=256):
    m, d = x.shape
    return pl.pallas_call(
        _rmsnorm_kernel,
        grid=(m // tm,),
        in_specs=[pl.BlockSpec((tm, d), lambda i: (i, 0)),
                  pl.BlockSpec((1, d), lambda i: (0, 0))],
        out_specs=pl.BlockSpec((tm, d), lambda i: (i, 0)),
        out_shape=jax.ShapeDtypeStruct((m, d), out_dtype),
        compiler_params=_cparams(("parallel",)),
        name="rmsnorm",
    )(x, w.reshape(1, d))


def _panel_spec(tm, k):
    return pl.BlockSpec((tm, k), lambda i, j: (i, 0), pipeline_mode=pl.Buffered(1))


def _norm_inputs_kernel(x_ref, xb_ref, ss_ref):
    x = x_ref[...]
    xb_ref[...] = x.astype(xb_ref.dtype)
    ss_ref[...] = jnp.broadcast_to(jnp.sum(x * x, axis=-1, keepdims=True), ss_ref.shape)


def norm_inputs(x, tm=256):
    m, d = x.shape
    return pl.pallas_call(
        _norm_inputs_kernel,
        grid=(m // tm,),
        in_specs=[pl.BlockSpec((tm, d), lambda i: (i, 0))],
        out_specs=[pl.BlockSpec((tm, d), lambda i: (i, 0)), pl.BlockSpec((tm, LANE), lambda i: (i, 0))],
        out_shape=[jax.ShapeDtypeStruct((m, d), BF16), jax.ShapeDtypeStruct((m, LANE), F32)],
        compiler_params=_cparams(("parallel",)),
        name="norm_inputs",
    )(x)


def _row_scale(ss_ref, d):
    return lax.rsqrt(ss_ref[...] / d + NORM_EPS)


def _reset_row_sums(ss_scr):
    @pl.when(pl.program_id(1) == 0)
    def _():
        ss_scr[...] = jnp.zeros_like(ss_scr)


def _emit_norm_inputs(x_new, xb_ref, ss_ref, ss_scr):
    xb_ref[...] = x_new.astype(xb_ref.dtype)
    sq = x_new * x_new
    part = ss_scr[...]
    for c0 in range(0, sq.shape[1], LANE):
        part = part + sq[:, c0:c0 + LANE]
    ss_scr[...] = part
    ss_ref[...] = jnp.broadcast_to(jnp.sum(part, axis=-1, keepdims=True), ss_ref.shape)


def _tile_dispatch(tile, tn, last_width):
    if last_width == tn:
        tile(tn)
    else:
        is_last = pl.program_id(1) == pl.num_programs(1) - 1
        pl.when(jnp.logical_not(is_last))(lambda: tile(tn))
        pl.when(is_last)(lambda: tile(last_width))


def _mm_in_kernel(a_ref, ss_ref, nw_ref, bt_ref, o_ref, *, last_width):
    r = _row_scale(ss_ref, a_ref.shape[1])

    def tile(width):
        bt = (bt_ref[:width, :] * nw_ref[...]).astype(BF16)
        acc = lax.dot_general(a_ref[...], bt, _NT, preferred_element_type=F32)
        for c0 in range(0, width, LANE):
            o_ref[:, c0:c0 + LANE] = acc[:, c0:c0 + LANE] * r

    _tile_dispatch(tile, o_ref.shape[1], last_width)


def matmul_in_proj(xb, ss, norm_w, w_in_t, layer, n_main, n_ab, n_pool, tm):
    m, k = xb.shape
    _, n, _ = w_in_t.shape
    main_blocks = n_main // IN_TILE
    ab_block = main_blocks + n_pool // IN_TILE

    def src_row(j):
        return jnp.where(j < main_blocks, j * IN_TILE,
                         jnp.where(j < ab_block, n_main + n_ab + (j - main_blocks) * IN_TILE, n_main))

    assert n_main + IN_TILE <= n
    return pl.pallas_call(
        functools.partial(_mm_in_kernel, last_width=LANE),
        grid=(m // tm, ab_block + 1),
        in_specs=[_panel_spec(tm, k),
                  pl.BlockSpec((tm, LANE), lambda i, j: (i, 0)),
                  pl.BlockSpec((1, k), lambda i, j: (0, 0)),
                  pl.BlockSpec((pl.Element(IN_TILE), pl.Element(k)),
                               lambda i, j: (pl.multiple_of(layer * n + src_row(j), SUBLANE), 0))],
        out_specs=pl.BlockSpec((tm, IN_TILE), lambda i, j: (i, j)),
        out_shape=jax.ShapeDtypeStruct((m, ab_block * IN_TILE + LANE), F32),
        compiler_params=_cparams(("parallel", "arbitrary")),
        name="in_proj",
    )(xb, ss, norm_w.reshape(1, k), w_in_t.reshape(-1, k))


def _norm_out_specs(tm, tn):
    return [pl.BlockSpec((tm, tn), lambda i, j: (i, j)),
            pl.BlockSpec((tm, tn), lambda i, j: (i, j)),
            pl.BlockSpec((tm, LANE), lambda i, j: (i, 0))]


def _norm_out_shapes(m, n):
    return [jax.ShapeDtypeStruct((m, n), F32), jax.ShapeDtypeStruct((m, n), BF16),
            jax.ShapeDtypeStruct((m, LANE), F32)]


def _mm_res_kernel(a_ref, b_ref, r_ref, o_ref, *norm_refs):
    if norm_refs:
        _reset_row_sums(norm_refs[-1])
    x_new = r_ref[...] + jnp.dot(a_ref[...], b_ref[...], preferred_element_type=F32)
    o_ref[...] = x_new
    if norm_refs:
        _emit_norm_inputs(x_new, *norm_refs)


def matmul_residual(a, b, res, tm, tn, k_splits, emit_norm_inputs):
    m, k = a.shape
    n = b.shape[1]
    kb = k // k_splits
    assert kb * k_splits == k and kb % LANE == 0
    x_spec = pl.BlockSpec((tm, tn), lambda i, j: (i, j))
    x = res
    for s in range(k_splits):
        emit = emit_norm_inputs and s == k_splits - 1
        x = pl.pallas_call(
            _mm_res_kernel,
            grid=(m // tm, n // tn),
            in_specs=[pl.BlockSpec((tm, kb), lambda i, j, s=s: (i, s)),
                      pl.BlockSpec((kb, tn), lambda i, j, s=s: (s, j)),
                      x_spec],
            out_specs=_norm_out_specs(tm, tn) if emit else x_spec,
            out_shape=_norm_out_shapes(m, n) if emit else jax.ShapeDtypeStruct((m, n), F32),
            scratch_shapes=[pltpu.VMEM((tm, LANE), F32)] if emit else [],
            compiler_params=_cparams(("parallel", "arbitrary")),
            name="matmul_residual",
        )(a, b, x)
    return x


def _layer_tile_spec(k, tn, layer):
    return pl.BlockSpec((None, k, tn), lambda i, j: (layer, 0, j))


def _mm4_res_kernel(a0, a1, a2, a3, b_ref, r_ref, o_ref, xb_ref, ss_ref, ss_scr):
    kg = a0.shape[1]
    _reset_row_sums(ss_scr)
    acc = r_ref[...]
    for g, a in enumerate((a0, a1, a2, a3)):
        acc = acc + jnp.dot(a[...], b_ref[g * kg:(g + 1) * kg, :], preferred_element_type=F32)
    o_ref[...] = acc
    _emit_norm_inputs(acc, xb_ref, ss_ref, ss_scr)


def matmul4_residual(parts, b, res, tm, tn):
    m, kg = parts[0].shape
    k, n = b.shape
    a_spec = pl.BlockSpec((tm, kg), lambda i, j: (i, 0))
    return pl.pallas_call(
        _mm4_res_kernel,
        grid=(m // tm, n // tn),
        in_specs=[a_spec, a_spec, a_spec, a_spec,
                  pl.BlockSpec((k, tn), lambda i, j: (0, j)),
                  pl.BlockSpec((tm, tn), lambda i, j: (i, j))],
        out_specs=_norm_out_specs(tm, tn),
        out_shape=_norm_out_shapes(m, n),
        scratch_shapes=[pltpu.VMEM((tm, LANE), F32)],
        compiler_params=_cparams(("parallel", "arbitrary")),
        name="out_proj",
    )(*parts, b, res)


def _mm_glu_kernel(a_ref, ss_ref, nw_ref, bg_ref, bu_ref, o_ref, *, last_width):
    r = _row_scale(ss_ref, a_ref.shape[1])

    def scaled(b_ref, width):
        return jnp.concatenate([b_ref[:, c0:c0 + LANE] * nw_ref[...] for c0 in range(0, width, LANE)],
                               axis=1).astype(BF16)

    def tile(width):
        a = a_ref[...]
        g = jnp.dot(a, scaled(bg_ref, width), preferred_element_type=F32)
        u = jnp.dot(a, scaled(bu_ref, width), preferred_element_type=F32)
        for c0 in range(0, width, LANE):
            gate = g[:, c0:c0 + LANE] * r
            o_ref[:, c0:c0 + LANE] = (_silu(gate) * (u[:, c0:c0 + LANE] * r)).astype(o_ref.dtype)

    _tile_dispatch(tile, o_ref.shape[1], last_width)


def matmul_swiglu(xb, ss, norm_w, wg_stacked, wu_stacked, layer, tm, tn):
    m, k = xb.shape
    n = wg_stacked.shape[2]
    b_spec = _layer_tile_spec(k, tn, layer)
    last_width = n - (pl.cdiv(n, tn) - 1) * tn
    return pl.pallas_call(
        functools.partial(_mm_glu_kernel, last_width=last_width),
        grid=(m // tm, pl.cdiv(n, tn)),
        in_specs=[_panel_spec(tm, k),
                  pl.BlockSpec((tm, LANE), lambda i, j: (i, 0)),
                  pl.BlockSpec((k, LANE), lambda i, j: (0, 0)),
                  b_spec, b_spec],
        out_specs=pl.BlockSpec((tm, tn), lambda i, j: (i, j)),
        out_shape=jax.ShapeDtypeStruct((m, n), BF16),
        compiler_params=_cparams(("parallel", "arbitrary")),
        name="ffn_gate_up",
    )(xb, ss, jnp.broadcast_to(norm_w.reshape(k, 1), (k, LANE)), wg_stacked, wu_stacked)


def _halo_spec(ts, hb, c, col):
    r = ts // hb
    return pl.BlockSpec((hb, c), lambda i: (jnp.maximum(i * r - 1, 0), col))


def _main_spec(ts, c, col):
    return pl.BlockSpec((ts, c), lambda i: (i, col))


def _fill_history(scr, first, halo, cur, hb):
    scr[0:hb, :] = jnp.where(first, 0.0, halo)
    scr[hb:, :] = cur


def _mix_sconv_kernel(b_ref, c_ref, h_ref, ch_ref, hh_ref, w_ref, o_ref, scr, *, tiles_per_seq):
    ts = o_ref.shape[0]
    first = (pl.program_id(0) % tiles_per_seq) == 0
    m = c_ref[...] * h_ref[...]
    _fill_history(scr, first, ch_ref[...] * hh_ref[...], m, SUBLANE)
    w = w_ref[...]
    kw = w.shape[0]
    y = w[kw - 1:kw, :] * m
    for d in range(1, kw):
        y = y + w[kw - 1 - d:kw - d, :] * scr[SUBLANE - d:SUBLANE - d + ts, :]
    o_ref[...] = (b_ref[...] * y).astype(o_ref.dtype)


def mix_sconv(sb, sc, sh, w, seq, ts=256):
    m = sb[0].shape[0]
    c = w.shape[1]
    kern = functools.partial(_mix_sconv_kernel, tiles_per_seq=seq // ts)
    return pl.pallas_call(
        kern,
        grid=(m // ts,),
        in_specs=[_main_spec(ts, c, sb[1]), _main_spec(ts, c, sc[1]), _main_spec(ts, c, sh[1]),
                  _halo_spec(ts, SUBLANE, c, sc[1]), _halo_spec(ts, SUBLANE, c, sh[1]),
                  pl.BlockSpec(w.shape, lambda i: (0, 0))],
        out_specs=pl.BlockSpec((ts, c), lambda i: (i, 0)),
        out_shape=jax.ShapeDtypeStruct((m, c), BF16),
        scratch_shapes=[pltpu.VMEM((SUBLANE + ts, c), F32)],
        compiler_params=_cparams(("parallel",)),
        name="mix_sconv",
    )(sb[0], sc[0], sh[0], sc[0], sh[0], w)


CF_HALO = 32


def _mix_conformer_kernel(v_ref, g_ref, vh_ref, gh_ref, w_ref, cb_ref, lw_ref, lb_ref, o_ref,
                          scr, sh_scr, cf_scr, *, tiles_per_seq):
    ts, c = o_ref.shape
    first = (pl.program_id(0) % tiles_per_seq) == 0
    glu = v_ref[...] * _sigmoid(g_ref[...])
    _fill_history(scr, first, vh_ref[...] * _sigmoid(gh_ref[...]), glu, CF_HALO)
    n_sh = sh_scr.shape[1]
    for r in range(1, SUBLANE):
        sh_scr[r - 1] = scr[SUBLANE - r:SUBLANE - r + n_sh, :]
    kw = w_ref.shape[0]
    s1 = jnp.zeros((ts, LANE), F32)
    for c0 in range(0, c, LANE):
        acc = jnp.zeros((ts, LANE), F32) + cb_ref[:, c0:c0 + LANE]
        for d in range(kw):
            a, r = divmod(d, SUBLANE)
            if r == 0:
                tap = scr[CF_HALO - d:CF_HALO - d + ts, c0:c0 + LANE]
            else:
                off = CF_HALO - SUBLANE * (a + 1)
                tap = sh_scr[r - 1, off:off + ts, c0:c0 + LANE]
            acc = acc + w_ref[kw - 1 - d:kw - d, c0:c0 + LANE] * tap
        cf_scr[:, c0:c0 + LANE] = acc
        s1 = s1 + acc
    mu = jnp.sum(s1, axis=-1, keepdims=True) * (1.0 / c)
    s2 = jnp.zeros((ts, LANE), F32)
    for c0 in range(0, c, LANE):
        xc = cf_scr[:, c0:c0 + LANE] - mu
        s2 = s2 + xc * xc
    rstd = lax.rsqrt(jnp.sum(s2, axis=-1, keepdims=True) * (1.0 / c) + NORM_EPS)
    for c0 in range(0, c, LANE):
        y = (cf_scr[:, c0:c0 + LANE] - mu) * rstd * lw_ref[:, c0:c0 + LANE] + lb_ref[:, c0:c0 + LANE]
        o_ref[:, c0:c0 + LANE] = _silu(y).astype(o_ref.dtype)


def mix_conformer(sv, sg, w, cb, lw, lb, seq, ts=128):
    m = sv[0].shape[0]
    c = w.shape[1]
    kern = functools.partial(_mix_conformer_kernel, tiles_per_seq=seq // ts)
    vec = pl.BlockSpec((1, c), lambda i: (0, 0))
    return pl.pallas_call(
        kern,
        grid=(m // ts,),
        in_specs=[_main_spec(ts, c, sv[1]), _main_spec(ts, c, sg[1]),
                  _halo_spec(ts, CF_HALO, c, sv[1]), _halo_spec(ts, CF_HALO, c, sg[1]),
                  pl.BlockSpec(w.shape, lambda i: (0, 0)), vec, vec, vec],
        out_specs=pl.BlockSpec((ts, c), lambda i: (i, 0)),
        out_shape=jax.ShapeDtypeStruct((m, c), BF16),
        scratch_shapes=[pltpu.VMEM((CF_HALO + ts, c), F32),
                        pltpu.VMEM((SUBLANE - 1, CF_HALO - SUBLANE + ts, c), F32),
                        pltpu.VMEM((ts, c), F32)],
        compiler_params=_cparams(("parallel",)),
        name="mix_conformer",
    )(sv[0], sg[0], sv[0], sg[0], w, cb.reshape(1, c), lw.reshape(1, c), lb.reshape(1, c))


POOL_HALO = 16


def _mix_pool_kernel(u_ref, uh_ref, pw_ref, ps_ref, o_ref, scr, *, tiles_per_seq):
    ts, c = o_ref.shape
    ng = pw_ref.shape[0]
    cg = c // ng
    tile = pl.program_id(0) % tiles_per_seq
    first = tile == 0
    _fill_history(scr, first, uh_ref[...], u_ref[...], POOL_HALO)
    pos = tile * ts + lax.broadcasted_iota(jnp.int32, (ts, 1), 0)
    for gi, win in enumerate(POOL_WINDOWS):
        c0 = gi * cg
        cur = scr[POOL_HALO:POOL_HALO + ts, c0:c0 + cg]
        acc = cur
        for d in range(1, win):
            acc = acc + scr[POOL_HALO - d:POOL_HALO - d + ts, c0:c0 + cg]
        cnt = jnp.minimum(pos + 1, win).astype(F32)
        p = acc / cnt - cur
        y = _dot_bf16(p, pw_ref[gi])
        o_ref[:, c0:c0 + cg] = (y * ps_ref[:, c0:c0 + cg]).astype(o_ref.dtype)


def mix_pool(su, pool_w, pool_scale, seq, ts=256):
    m = su[0].shape[0]
    ng, cg, _ = pool_w.shape
    c = ng * cg
    kern = functools.partial(_mix_pool_kernel, tiles_per_seq=seq // ts)
    return pl.pallas_call(
        kern,
        grid=(m // ts,),
        in_specs=[_main_spec(ts, c, su[1]), _halo_spec(ts, POOL_HALO, c, su[1]),
                  pl.BlockSpec(pool_w.shape, lambda i: (0, 0, 0)),
                  pl.BlockSpec((1, c), lambda i: (0, 0))],
        out_specs=pl.BlockSpec((ts, c), lambda i: (i, 0)),
        out_shape=jax.ShapeDtypeStruct((m, c), BF16),
        scratch_shapes=[pltpu.VMEM((POOL_HALO + ts, c), F32)],
        compiler_params=_cparams(("parallel",)),
        name="mix_pool",
    )(su[0], su[0], pool_w.astype(BF16), pool_scale.reshape(1, c))


def _split3(x):
    x1 = x.astype(BF16)
    r1 = x - x1.astype(F32)
    x2 = r1.astype(BF16)
    x3 = (r1 - x2.astype(F32)).astype(BF16)
    return x1, x2, x3


INV_BLOCK = 16
P1_GROUP = 16


def _gdn_kernel(q_ref, k_ref, v_ref, qh_ref, kh_ref, vh_ref, z_ref, ab_ref,
                wq_ref, wk_ref, wv_ref, alog_ref, dtb_ref, nw_ref, o_ref,
                st_ref, hist, qs, ks, vs, gc_s, gcr_s, kn_s, kbq_s, vk_s, wqd_s, kd_s, u_s, a_s,
                *, n_heads):
    ts = o_ref.shape[0]
    dh = GDN_HEAD_DIM
    cc = GDN_CHUNK
    n_chunks = ts // cc
    first = pl.program_id(1) == 0

    @pl.when(first)
    def _():
        st_ref[...] = jnp.zeros_like(st_ref)

    def conv_silu(x_ref, xh_ref, w_ref, dst):
        _fill_history(hist, first, xh_ref[...], x_ref[...], SUBLANE)
        kw = w_ref.shape[0]
        acc = w_ref[kw - 1:kw, :] * x_ref[...]
        for d in range(1, kw):
            acc = acc + w_ref[kw - 1 - d:kw - d, :] * hist[SUBLANE - d:SUBLANE - d + ts, :]
        dst[...] = _silu(acc)

    conv_silu(q_ref, qh_ref, wq_ref, qs)
    conv_silu(k_ref, kh_ref, wk_ref, ks)
    conv_silu(v_ref, vh_ref, wv_ref, vs)

    ab = ab_ref[...]
    xg = ab + dtb_ref[...]
    softplus = jnp.maximum(xg, 0.0) + jnp.log1p(jnp.exp(-jnp.abs(xg)))
    g_all = -jnp.exp(alog_ref[...]) * softplus
    beta_all = _sigmoid(ab)

    rt = lax.broadcasted_iota(jnp.int32, (ts, ts), 0)
    ct = lax.broadcasted_iota(jnp.int32, (ts, ts), 1)
    tri_blk = jnp.where((rt >= ct) & ((rt // cc) == (ct // cc)), 1.0, 0.0).astype(BF16)
    g1, g2, g3 = _split3(g_all)
    gc_all = (jnp.dot(tri_blk, g1, preferred_element_type=F32)
              + jnp.dot(tri_blk, g2, preferred_element_type=F32)
              + jnp.dot(tri_blk, g3, preferred_element_type=F32))
    gc_s[...] = gc_all
    gcr_s[...] = gc_all.T

    scale = dh ** -0.5

    for h in range(n_heads):
        lanes = slice(h * dh, (h + 1) * dh)
        gc_col = jnp.broadcast_to(gc_all[:, h:h + 1], (ts, dh))
        b_col = jnp.broadcast_to(beta_all[:, n_heads + h:n_heads + h + 1], (ts, dh))
        q_all = qs[:, lanes]
        k_all = ks[:, lanes]
        q_all = q_all * (lax.rsqrt(jnp.sum(q_all * q_all, axis=-1, keepdims=True) + NORM_EPS) * scale)
        k_all = k_all * lax.rsqrt(jnp.sum(k_all * k_all, axis=-1, keepdims=True) + NORM_EPS)
        egc_all = jnp.exp(gc_col)
        kb_all = k_all * b_col
        kn_s[:, lanes] = k_all.astype(BF16)
        vk_s[:, 2 * h * dh:(2 * h + 1) * dh] = (vs[:, lanes] * b_col).astype(BF16)
        vk_s[:, (2 * h + 1) * dh:(2 * h + 2) * dh] = (kb_all * egc_all).astype(BF16)
        qd_all = q_all * egc_all
        for c in range(n_chunks):
            rows = slice(c * cc, (c + 1) * cc)
            gc_last = gc_col[(c + 1) * cc - 1:(c + 1) * cc, :]
            kbq_s[2 * c * cc:(2 * c + 1) * cc, lanes] = kb_all[rows].astype(BF16)
            kbq_s[(2 * c + 1) * cc:(2 * c + 2) * cc, lanes] = q_all[rows].astype(BF16)
            wqd_s[(2 * c + 1) * cc:(2 * c + 2) * cc, lanes] = qd_all[rows].astype(BF16)
            kd_s[rows, lanes] = (k_all[rows] * jnp.exp(gc_last - gc_col[rows])).astype(BF16)

    row = lax.broadcasted_iota(jnp.int32, (cc, cc), 0)
    col = lax.broadcasted_iota(jnp.int32, (cc, cc), 1)
    eye = (row == col).astype(F32)
    assert cc == 4 * INV_BLOCK
    in_diag_block = (row // INV_BLOCK) == (col // INV_BLOCK)

    def p1_stages(c, h):
        rows = slice(c * cc, (c + 1) * cc)
        lanes = slice(h * dh, (h + 1) * dh)
        s = {}

        def s0():
            s["kq"] = lax.dot_general(kbq_s[2 * c * cc:(2 * c + 2) * cc, lanes], kn_s[rows, lanes], _NT,
                                      preferred_element_type=F32)

        def s1():
            pdiff = jnp.broadcast_to(gc_s[rows, h:h + 1], (cc, cc)) - gcr_s[h:h + 1, rows]
            decay = jnp.exp(jnp.where(row >= col, pdiff, -jnp.inf))
            kq = s.pop("kq")
            m = jnp.where(row > col, kq[:cc] * decay, 0.0)
            a_s[rows, h * cc:(h + 1) * cc] = (kq[cc:] * decay).astype(BF16)
            s["d"] = jnp.where(in_diag_block, m, 0.0)
            s["n"] = jnp.where(in_diag_block, 0.0, m).astype(BF16)
            s["d1"] = s["d"].astype(BF16)
            s["d2"] = _dot_bf16(s["d1"], s["d1"])

        def s2():
            d2b = s["d2"].astype(BF16)
            s["d4"] = _dot_bf16(d2b, d2b)
            s["d3"] = _dot_bf16(s.pop("d1"), d2b)

        def s3():
            s["d4b"] = s["d4"].astype(BF16)
            s["d8"] = _dot_bf16(s["d4b"], s["d4b"])
            s["x1"] = eye - s.pop("d") + s.pop("d2") - s.pop("d3")

        def s4():
            s["d12"] = _dot_bf16(s.pop("d4b"), s["d8"])

        def s5():
            x2 = eye + s.pop("d4") + s.pop("d8") + s.pop("d12")
            s["td"] = _dot_bf16(s.pop("x1"), x2).astype(BF16)

        def s6():
            s["q"] = _dot_bf16(s["td"], s.pop("n"))

        def s7():
            s["qb"] = s["q"].astype(BF16)
            s["q2"] = _dot_bf16(s["qb"], s["qb"])

        def s8():
            s["q3"] = _dot_bf16(s.pop("qb"), s["q2"])

        def s9():
            t_left = eye - s.pop("q") + s.pop("q2") - s.pop("q3")
            s["t"] = _dot_bf16(t_left, s.pop("td"))

        def s10():
            uw = _dot_bf16(s.pop("t"), vk_s[rows, 2 * h * dh:(2 * h + 2) * dh])
            u_s[rows, lanes] = uw[:, :dh]
            wqd_s[2 * c * cc:(2 * c + 1) * cc, lanes] = uw[:, dh:].astype(BF16)

        return (s0, s1, s2, s3, s4, s5, s6, s7, s8, s9, s10)

    chains = [(c, h) for c in range(n_chunks) for h in range(n_heads)]
    for g0 in range(0, len(chains), P1_GROUP):
        group = [p1_stages(c, h) for c, h in chains[g0:g0 + P1_GROUP]]
        for stage in zip(*group):
            for fn in stage:
                fn()

    for c in range(n_chunks):
        rows = slice(c * cc, (c + 1) * cc)
        states = [st_ref[h] for h in range(n_heads)]
        ws_qs = [lax.dot_general(wqd_s[2 * c * cc:(2 * c + 2) * cc, h * dh:(h + 1) * dh],
                                 states[h].astype(BF16), _NN, preferred_element_type=F32)
                 for h in range(n_heads)]
        v_new = [(u_s[rows, h * dh:(h + 1) * dh] - ws_qs[h][:cc]).astype(BF16) for h in range(n_heads)]
        o_intra = [lax.dot_general(a_s[rows, h * cc:(h + 1) * cc], v_new[h], _NN, preferred_element_type=F32)
                   for h in range(n_heads)]
        s_upd = [lax.dot_general(kd_s[rows, h * dh:(h + 1) * dh], v_new[h], _TN, preferred_element_type=F32)
                 for h in range(n_heads)]
        for h in range(n_heads):
            lanes = slice(h * dh, (h + 1) * dh)
            g_last = jnp.exp(jnp.broadcast_to(gc_s[(c + 1) * cc - 1:(c + 1) * cc, h:h + 1], (1, dh)))
            st_ref[h] = states[h] * g_last + s_upd[h]
            o = ws_qs[h][cc:] + o_intra[h]
            o = o * lax.rsqrt(jnp.mean(o * o, axis=-1, keepdims=True) + NORM_EPS) * nw_ref[...]
            o_ref[rows, lanes] = (o * _silu(z_ref[rows, lanes])).astype(o_ref.dtype)


def mix_gdn(sq, sk, sv, sz, sab, conv_w, a_log, dt_bias, norm_w, batch, seq, ts=256):
    m = sq[0].shape[0]
    c = conv_w.shape[1] // 3
    n_heads = c // GDN_HEAD_DIM
    nt = seq // ts
    r = ts // SUBLANE

    def main(colblk):
        return pl.BlockSpec((ts, c), lambda b, s: (b * nt + s, colblk))

    def halo(colblk):
        return pl.BlockSpec((SUBLANE, c), lambda b, s: (jnp.maximum((b * nt + s) * r - 1, 0), colblk))

    def wspec(j):
        return pl.BlockSpec((conv_w.shape[0], c), lambda b, s: (0, j))

    vec = pl.BlockSpec((1, LANE), lambda b, s: (0, 0))
    pad = LANE - n_heads
    alog = jnp.pad(a_log, (0, pad)).reshape(1, LANE)
    dtb = jnp.pad(dt_bias, (0, pad)).reshape(1, LANE)
    kern = functools.partial(_gdn_kernel, n_heads=n_heads)
    return pl.pallas_call(
        kern,
        grid=(batch, nt),
        in_specs=[main(sq[1]), main(sk[1]), main(sv[1]),
                  halo(sq[1]), halo(sk[1]), halo(sv[1]),
                  main(sz[1]),
                  pl.BlockSpec((ts, LANE), lambda b, s: (b * nt + s, sab[1])),
                  wspec(0), wspec(1), wspec(2), vec, vec, vec],
        out_specs=pl.BlockSpec((ts, c), lambda b, s: (b * nt + s, 0)),
        out_shape=jax.ShapeDtypeStruct((m, c), BF16),
        scratch_shapes=[pltpu.VMEM((n_heads, GDN_HEAD_DIM, GDN_HEAD_DIM), F32),
                        pltpu.VMEM((SUBLANE + ts, c), F32),
                        pltpu.VMEM((ts, c), F32),
                        pltpu.VMEM((ts, c), F32),
                        pltpu.VMEM((ts, c), F32),
                        pltpu.VMEM((ts, LANE), F32),
                        pltpu.VMEM((LANE, ts), F32),
                        pltpu.VMEM((ts, c), BF16),
                        pltpu.VMEM((2 * ts, c), BF16),
                        pltpu.VMEM((ts, 2 * c), BF16),
                        pltpu.VMEM((2 * ts, c), BF16),
                        pltpu.VMEM((ts, c), BF16),
                        pltpu.VMEM((ts, c), F32),
                        pltpu.VMEM((ts, n_heads * GDN_CHUNK), BF16)],
        compiler_params=_cparams(("arbitrary", "arbitrary")),
        name="mix_gdn",
    )(sq[0], sk[0], sv[0], sq[0], sk[0], sv[0], sz[0], sab[0],
      conv_w, conv_w, conv_w, alog, dtb, norm_w.reshape(1, LANE))


def _layer(x2, xb, ss, batch, seq, li, p, stacked, last):
    cg = p["sc_conv_w"].shape[1]
    n_heads = cg // GDN_HEAD_DIM
    n_main = 9 * cg
    n_ab = 2 * n_heads

    proj = matmul_in_proj(xb, ss, p["attn_norm_w"], stacked["w_in_t"], li, n_main, n_ab, cg, tm=IN_ROWS)

    y_a = mix_sconv((proj, 0), (proj, 1), (proj, 2), p["sc_conv_w"], seq)
    y_b = mix_conformer((proj, 3), (proj, 4), p["cf_conv_w"], p["cf_conv_b"], p["cf_ln_w"], p["cf_ln_b"], seq)
    y_c = mix_gdn((proj, 5), (proj, 6), (proj, 7), (proj, 8), (proj, (n_main + cg) // LANE),
                  p["gdn_conv_w"], p["gdn_a_log"], p["gdn_dt_bias"], p["gdn_norm_w"], batch, seq)
    y_d = mix_pool((proj, 9), p["pool_w"], p["pool_scale"], seq)

    x2, xb, ss = matmul4_residual((y_a, y_b, y_c, y_d), cast_weight(stacked["w_out"], li), x2,
                                  tm=OUT_ROWS, tn=OUT_TILE)

    wd = cast_weight(stacked["w_down"], li)
    act = matmul_swiglu(xb, ss, p["ffn_norm_w"], stacked["w_gate"], stacked["w_up"], li,
                        tm=FF_ROWS, tn=FF_TILE)
    if last:
        return matmul_residual(act, wd, x2, DOWN_ROWS, DOWN_TILE, DOWN_K_SPLITS, emit_norm_inputs=False), None, None
    return matmul_residual(act, wd, x2, DOWN_ROWS, DOWN_TILE, DOWN_K_SPLITS, emit_norm_inputs=True)


def kernel(x, attn_norm_w, w_in, sc_conv_w, cf_conv_w, cf_conv_b, cf_ln_w, cf_ln_b, gdn_conv_w,
           gdn_a_log, gdn_dt_bias, gdn_norm_w, pool_w, pool_scale, w_out, ffn_norm_w, w_gate, w_up,
           w_down, final_norm_w):
    batch, seq, d = x.shape
    small = dict(attn_norm_w=attn_norm_w, sc_conv_w=sc_conv_w, cf_conv_w=cf_conv_w,
                 cf_conv_b=cf_conv_b, cf_ln_w=cf_ln_w, cf_ln_b=cf_ln_b, gdn_conv_w=gdn_conv_w,
                 gdn_a_log=gdn_a_log, gdn_dt_bias=gdn_dt_bias, gdn_norm_w=gdn_norm_w, pool_w=pool_w,
                 pool_scale=pool_scale, ffn_norm_w=ffn_norm_w)
    stacked = dict(w_in_t=jnp.swapaxes(w_in, 1, 2), w_out=w_out, w_gate=w_gate, w_up=w_up, w_down=w_down)
    x2 = x.reshape(batch * seq, d)
    xb, ss = norm_inputs(x2)
    depth = attn_norm_w.shape[0]
    for li in range(depth):
        x2, xb, ss = _layer(x2, xb, ss, batch, seq, li, {name: val[li] for name, val in small.items()},
                            stacked, last=li == depth - 1)
    out = rmsnorm(x2, final_norm_w, x.dtype)
    return out.reshape(batch, seq, d)
```

```python
import functools

import jax
import jax.numpy as jnp
from jax import lax
from jax.experimental import pallas as pl
from jax.experimental.pallas import tpu as pltpu

NORM_EPS = 1e-6
GDN_HEAD_DIM = 128
GDN_CHUNK = 64
POOL_WINDOWS = (2, 4, 8, 16)
LANE = 128
SUBLANE = 8
VMEM_LIMIT = 56 * 1024 * 1024

IN_ROWS, IN_TILE = 2048, 512
OUT_ROWS, OUT_TILE = 1024, 512
FF_ROWS, FF_TILE = 1024, 512
DOWN_ROWS, DOWN_TILE, DOWN_K_SPLITS = 1024, 512, 2

BF16 = jnp.bfloat16
F32 = jnp.float32


def _cparams(sem):
    return pltpu.CompilerParams(dimension_semantics=sem, vmem_limit_bytes=VMEM_LIMIT)


def _sigmoid(x):
    return 1.0 / (1.0 + jnp.exp(-x))


def _silu(x):
    return x * _sigmoid(x)


_NN = (((1,), (0,)), ((), ()))
_NT = (((1,), (1,)), ((), ()))
_TN = (((0,), (0,)), ((), ()))


def _dot_bf16(a, b, dims=_NN):
    return lax.dot_general(a.astype(BF16), b.astype(BF16), dims, preferred_element_type=F32)


def _cast_kernel(x_ref, o_ref):
    o_ref[...] = x_ref[...].astype(o_ref.dtype)


def cast_weight(w_stacked, layer, tk=128):
    _, k, n = w_stacked.shape
    return pl.pallas_call(
        _cast_kernel,
        grid=(k // tk,),
        in_specs=[pl.BlockSpec((None, tk, n), lambda r: (layer, r, 0))],
        out_specs=pl.BlockSpec((tk, n), lambda r: (r, 0)),
        out_shape=jax.ShapeDtypeStruct((k, n), BF16),
        compiler_params=_cparams(("parallel",)),
        name="cast_weight",
    )(w_stacked)


def _rmsnorm_kernel(x_ref, w_ref, o_ref):
    x = x_ref[...]
    ms = jnp.mean(x * x, axis=-1, keepdims=True)
    o_ref[...] = (x * lax.rsqrt(ms + NORM_EPS) * w_ref[...]).astype(o_ref.dtype)


def rmsnorm(x, w, out_dtype, tm=256):
    m, d = x.shape
    return pl.pallas_call(
        _rmsnorm_kernel,
        grid=(m // tm,),
        in_specs=[pl.BlockSpec((tm, d), lambda i: (i, 0)),
                  pl.BlockSpec((1, d), lambda i: (0, 0))],
        out_specs=pl.BlockSpec((tm, d), lambda i: (i, 0)),
        out_shape=jax.ShapeDtypeStruct((m, d), out_dtype),
        compiler_params=_cparams(("parallel",)),
        name="rmsnorm",
    )(x, w.reshape(1, d))


def _panel_spec(tm, k):
    return pl.BlockSpec((tm, k), lambda i, j: (i, 0), pipeline_mode=pl.Buffered(1))


def _norm_inputs_kernel(x_ref, xb_ref, ss_ref):
    x = x_ref[...]
    xb_ref[...] = x.astype(xb_ref.dtype)
    ss_ref[...] = jnp.broadcast_to(jnp.sum(x * x, axis=-1, keepdims=True), ss_ref.shape)


def norm_inputs(x, tm=256):
    m, d = x.shape
    return pl.pallas_call(
        _norm_inputs_kernel,
        grid=(m // tm,),
        in_specs=[pl.BlockSpec((tm, d), lambda i: (i, 0))],
        out_specs=[pl.BlockSpec((tm, d), lambda i: (i, 0)), pl.BlockSpec((tm, LANE), lambda i: (i, 0))],
        out_shape=[jax.ShapeDtypeStruct((m, d), BF16), jax.ShapeDtypeStruct((m, LANE), F32)],
        compiler_params=_cparams(("parallel",)),
        name="norm_inputs",
    )(x)


def _row_scale(ss_ref, d):
    return lax.rsqrt(ss_ref[...] / d + NORM_EPS)


def _reset_row_sums(ss_scr):
    @pl.when(pl.program_id(1) == 0)
    def _():
        ss_scr[...] = jnp.zeros_like(ss_scr)


def _emit_norm_inputs(x_new, xb_ref, ss_ref, ss_scr):
    xb_ref[...] = x_new.astype(xb_ref.dtype)
    sq = x_new * x_new
    part = ss_scr[...]
    for c0 in range(0, sq.shape[1], LANE):
        part = part + sq[:, c0:c0 + LANE]
    ss_scr[...] = part
    ss_ref[...] = jnp.broadcast_to(jnp.sum(part, axis=-1, keepdims=True), ss_ref.shape)


def _tile_dispatch(tile, tn, last_width):
    if last_width == tn:
        tile(tn)
    else:
        is_last = pl.program_id(1) == pl.num_programs(1) - 1
        pl.when(jnp.logical_not(is_last))(lambda: tile(tn))
        pl.when(is_last)(lambda: tile(last_width))


def _mm_in_kernel(a_ref, ss_ref, nw_ref, bt_ref, o_ref, *, last_width):
    r = _row_scale(ss_ref, a_ref.shape[1])

    def tile(width):
        bt = (bt_ref[:width, :] * nw_ref[...]).astype(BF16)
        acc = lax.dot_general(a_ref[...], bt, _NT, preferred_element_type=F32)
        for c0 in range(0, width, LANE):
            o_ref[:, c0:c0 + LANE] = acc[:, c0:c0 + LANE] * r

    _tile_dispatch(tile, o_ref.shape[1], last_width)


def matmul_in_proj(xb, ss, norm_w, w_in_t, layer, n_main, n_ab, n_pool, tm):
    m, k = xb.shape
    _, n, _ = w_in_t.shape
    main_blocks = n_main // IN_TILE
    ab_block = main_blocks + n_pool // IN_TILE

    def src_row(j):
        return jnp.where(j < main_blocks, j * IN_TILE,
                         jnp.where(j < ab_block, n_main + n_ab + (j - main_blocks) * IN_TILE, n_main))

    assert n_main + IN_TILE <= n
    return pl.pallas_call(
        functools.partial(_mm_in_kernel, last_width=LANE),
        grid=(m // tm, ab_block + 1),
        in_specs=[_panel_spec(tm, k),
                  pl.BlockSpec((tm, LANE), lambda i, j: (i, 0)),
                  pl.BlockSpec((1, k), lambda i, j: (0, 0)),
                  pl.BlockSpec((pl.Element(IN_TILE), pl.Element(k)),
                               lambda i, j: (pl.multiple_of(layer * n + src_row(j), SUBLANE), 0))],
        out_specs=pl.BlockSpec((tm, IN_TILE), lambda i, j: (i, j)),
        out_shape=jax.ShapeDtypeStruct((m, ab_block * IN_TILE + LANE), F32),
        compiler_params=_cparams(("parallel", "arbitrary")),
        name="in_proj",
    )(xb, ss, norm_w.reshape(1, k), w_in_t.reshape(-1, k))


def _norm_out_specs(tm, tn):
    return [pl.BlockSpec((tm, tn), lambda i, j: (i, j)),
            pl.BlockSpec((tm, tn), lambda i, j: (i, j)),
            pl.BlockSpec((tm, LANE), lambda i, j: (i, 0))]


def _norm_out_shapes(m, n):
    return [jax.ShapeDtypeStruct((m, n), F32), jax.ShapeDtypeStruct((m, n), BF16),
            jax.ShapeDtypeStruct((m, LANE), F32)]


def _mm_res_kernel(a_ref, b_ref, r_ref, o_ref, *norm_refs):
    if norm_refs:
        _reset_row_sums(norm_refs[-1])
    x_new = r_ref[...] + jnp.dot(a_ref[...], b_ref[...], preferred_element_type=F32)
    o_ref[...] = x_new
    if norm_refs:
        _emit_norm_inputs(x_new, *norm_refs)


def matmul_residual(a, b, res, tm, tn, k_splits, emit_norm_inputs):
    m, k = a.shape
    n = b.shape[1]
    kb = k // k_splits
    assert kb * k_splits == k and kb % LANE == 0
    x_spec = pl.BlockSpec((tm, tn), lambda i, j: (i, j))
    x = res
    for s in range(k_splits):
        emit = emit_norm_inputs and s == k_splits - 1
        x = pl.pallas_call(
            _mm_res_kernel,
            grid=(m // tm, n // tn),
            in_specs=[pl.BlockSpec((tm, kb), lambda i, j, s=s: (i, s)),
                      pl.BlockSpec((kb, tn), lambda i, j, s=s: (s, j)),
                      x_spec],
            out_specs=_norm_out_specs(tm, tn) if emit else x_spec,
            out_shape=_norm_out_shapes(m, n) if emit else jax.ShapeDtypeStruct((m, n), F32),
            scratch_shapes=[pltpu.VMEM((tm, LANE), F32)] if emit else [],
            compiler_params=_cparams(("parallel", "arbitrary")),
            name="matmul_residual",
        )(a, b, x)
    return x


def _layer_tile_spec(k, tn, layer):
    return pl.BlockSpec((None, k, tn), lambda i, j: (layer, 0, j))


def _mm4_res_kernel(a0, a1, a2, a3, b_ref, r_ref, o_ref, xb_ref, ss_ref, ss_scr):
    kg = a0.shape[1]
    _reset_row_sums(ss_scr)
    acc = r_ref[...]
    for g, a in enumerate((a0, a1, a2, a3)):
        acc = acc + jnp.dot(a[...], b_ref[g * kg:(g + 1) * kg, :], preferred_element_type=F32)
    o_ref[...] = acc
    _emit_norm_inputs(acc, xb_ref, ss_ref, ss_scr)


def matmul4_residual(parts, b, res, tm, tn):
    m, kg = parts[0].shape
    k, n = b.shape
    a_spec = pl.BlockSpec((tm, kg), lambda i, j: (i, 0))
    return pl.pallas_call(
        _mm4_res_kernel,
        grid=(m // tm, n // tn),
        in_specs=[a_spec, a_spec, a_spec, a_spec,
                  pl.BlockSpec((k, tn), lambda i, j: (0, j)),
                  pl.BlockSpec((tm, tn), lambda i, j: (i, j))],
        out_specs=_norm_out_specs(tm, tn),
        out_shape=_norm_out_shapes(m, n),
        scratch_shapes=[pltpu.VMEM((tm, LANE), F32)],
        compiler_params=_cparams(("parallel", "arbitrary")),
        name="out_proj",
    )(*parts, b, res)


def _mm_glu_kernel(a_ref, ss_ref, nw_ref, bg_ref, bu_ref, o_ref, *, last_width):
    r = _row_scale(ss_ref, a_ref.shape[1])

    def scaled(b_ref, width):
        return jnp.concatenate([b_ref[:, c0:c0 + LANE] * nw_ref[...] for c0 in range(0, width, LANE)],
                               axis=1).astype(BF16)

    def tile(width):
        a = a_ref[...]
        g = jnp.dot(a, scaled(bg_ref, width), preferred_element_type=F32)
        u = jnp.dot(a, scaled(bu_ref, width), preferred_element_type=F32)
        for c0 in range(0, width, LANE):
            gate = g[:, c0:c0 + LANE] * r
            o_ref[:, c0:c0 + LANE] = (_silu(gate) * (u[:, c0:c0 + LANE] * r)).astype(o_ref.dtype)

    _tile_dispatch(tile, o_ref.shape[1], last_width)


def matmul_swiglu(xb, ss, norm_w, wg_stacked, wu_stacked, layer, tm, tn):
    m, k = xb.shape
    n = wg_stacked.shape[2]
    b_spec = _layer_tile_spec(k, tn, layer)
    last_width = n - (pl.cdiv(n, tn) - 1) * tn
    return pl.pallas_call(
        functools.partial(_mm_glu_kernel, last_width=last_width),
        grid=(m // tm, pl.cdiv(n, tn)),
        in_specs=[_panel_spec(tm, k),
                  pl.BlockSpec((tm, LANE), lambda i, j: (i, 0)),
                  pl.BlockSpec((k, LANE), lambda i, j: (0, 0)),
                  b_spec, b_spec],
        out_specs=pl.BlockSpec((tm, tn), lambda i, j: (i, j)),
        out_shape=jax.ShapeDtypeStruct((m, n), BF16),
        compiler_params=_cparams(("parallel", "arbitrary")),
        name="ffn_gate_up",
    )(xb, ss, jnp.broadcast_to(norm_w.reshape(k, 1), (k, LANE)), wg_stacked, wu_stacked)


def _halo_spec(ts, hb, c, col):
    r = ts // hb
    return pl.BlockSpec((hb, c), lambda i: (jnp.maximum(i * r - 1, 0), col))


def _main_spec(ts, c, col):
    return pl.BlockSpec((ts, c), lambda i: (i, col))


def _fill_history(scr, first, halo, cur, hb):
    scr[0:hb, :] = jnp.where(first, 0.0, halo)
    scr[hb:, :] = cur


def _mix_sconv_kernel(b_ref, c_ref, h_ref, ch_ref, hh_ref, w_ref, o_ref, scr, *, tiles_per_seq):
    ts = o_ref.shape[0]
    first = (pl.program_id(0) % tiles_per_seq) == 0
    m = c_ref[...] * h_ref[...]
    _fill_history(scr, first, ch_ref[...] * hh_ref[...], m, SUBLANE)
    w = w_ref[...]
    kw = w.shape[0]
    y = w[kw - 1:kw, :] * m
    for d in range(1, kw):
        y = y + w[kw - 1 - d:kw - d, :] * scr[SUBLANE - d:SUBLANE - d + ts, :]
    o_ref[...] = (b_ref[...] * y).astype(o_ref.dtype)


def mix_sconv(sb, sc, sh, w, seq, ts=256):
    m = sb[0].shape[0]
    c = w.shape[1]
    kern = functools.partial(_mix_sconv_kernel, tiles_per_seq=seq // ts)
    return pl.pallas_call(
        kern,
        grid=(m // ts,),
        in_specs=[_main_spec(ts, c, sb[1]), _main_spec(ts, c, sc[1]), _main_spec(ts, c, sh[1]),
                  _halo_spec(ts, SUBLANE, c, sc[1]), _halo_spec(ts, SUBLANE, c, sh[1]),
                  pl.BlockSpec(w.shape, lambda i: (0, 0))],
        out_specs=pl.BlockSpec((ts, c), lambda i: (i, 0)),
        out_shape=jax.ShapeDtypeStruct((m, c), BF16),
        scratch_shapes=[pltpu.VMEM((SUBLANE + ts, c), F32)],
        compiler_params=_cparams(("parallel",)),
        name="mix_sconv",
    )(sb[0], sc[0], sh[0], sc[0], sh[0], w)


CF_HALO = 32


def _mix_conformer_kernel(v_ref, g_ref, vh_ref, gh_ref, w_ref, cb_ref, lw_ref, lb_ref, o_ref,
                          scr, sh_scr, cf_scr, *, tiles_per_seq):
    ts, c = o_ref.shape
    first = (pl.program_id(0) % tiles_per_seq) == 0
    glu = v_ref[...] * _sigmoid(g_ref[...])
    _fill_history(scr, first, vh_ref[...] * _sigmoid(gh_ref[...]), glu, CF_HALO)
    n_sh = sh_scr.shape[1]
    for r in range(1, SUBLANE):
        sh_scr[r - 1] = scr[SUBLANE - r:SUBLANE - r + n_sh, :]
    kw = w_ref.shape[0]
    s1 = jnp.zeros((ts, LANE), F32)
    for c0 in range(0, c, LANE):
        acc = jnp.zeros((ts, LANE), F32) + cb_ref[:, c0:c0 + LANE]
        for d in range(kw):
            a, r = divmod(d, SUBLANE)
            if r == 0:
                tap = scr[CF_HALO - d:CF_HALO - d + ts, c0:c0 + LANE]
            else:
                off = CF_HALO - SUBLANE * (a + 1)
                tap = sh_scr[r - 1, off:off + ts, c0:c0 + LANE]
            acc = acc + w_ref[kw - 1 - d:kw - d, c0:c0 + LANE] * tap
        cf_scr[:, c0:c0 + LANE] = acc
        s1 = s1 + acc
    mu = jnp.sum(s1, axis=-1, keepdims=True) * (1.0 / c)
    s2 = jnp.zeros((ts, LANE), F32)
    for c0 in range(0, c, LANE):
        xc = cf_scr[:, c0:c0 + LANE] - mu
        s2 = s2 + xc * xc
    rstd = lax.rsqrt(jnp.sum(s2, axis=-1, keepdims=True) * (1.0 / c) + NORM_EPS)
    for c0 in range(0, c, LANE):
        y = (cf_scr[:, c0:c0 + LANE] - mu) * rstd * lw_ref[:, c0:c0 + LANE] + lb_ref[:, c0:c0 + LANE]
        o_ref[:, c0:c0 + LANE] = _silu(y).astype(o_ref.dtype)


def mix_conformer(sv, sg, w, cb, lw, lb, seq, ts=128):
    m = sv[0].shape[0]
    c = w.shape[1]
    kern = functools.partial(_mix_conformer_kernel, tiles_per_seq=seq // ts)
    vec = pl.BlockSpec((1, c), lambda i: (0, 0))
    return pl.pallas_call(
        kern,
        grid=(m // ts,),
        in_specs=[_main_spec(ts, c, sv[1]), _main_spec(ts, c, sg[1]),
                  _halo_spec(ts, CF_HALO, c, sv[1]), _halo_spec(ts, CF_HALO, c, sg[1]),
                  pl.BlockSpec(w.shape, lambda i: (0, 0)), vec, vec, vec],
        out_specs=pl.BlockSpec((ts, c), lambda i: (i, 0)),
        out_shape=jax.ShapeDtypeStruct((m, c), BF16),
        scratch_shapes=[pltpu.VMEM((CF_HALO + ts, c), F32),
                        pltpu.VMEM((SUBLANE - 1, CF_HALO - SUBLANE + ts, c), F32),
                        pltpu.VMEM((ts, c), F32)],
        compiler_params=_cparams(("parallel",)),
        name="mix_conformer",
    )(sv[0], sg[0], sv[0], sg[0], w, cb.reshape(1, c), lw.reshape(1, c), lb.reshape(1, c))


POOL_HALO = 16


def _mix_pool_kernel(u_ref, uh_ref, pw_ref, ps_ref, o_ref, scr, *, tiles_per_seq):
    ts, c = o_ref.shape
    ng = pw_ref.shape[0]
    cg = c // ng
    tile = pl.program_id(0) % tiles_per_seq
    first = tile == 0
    _fill_history(scr, first, uh_ref[...], u_ref[...], POOL_HALO)
    pos = tile * ts + lax.broadcasted_iota(jnp.int32, (ts, 1), 0)
    for gi, win in enumerate(POOL_WINDOWS):
        c0 = gi * cg
        cur = scr[POOL_HALO:POOL_HALO + ts, c0:c0 + cg]
        acc = cur
        for d in range(1, win):
            acc = acc + scr[POOL_HALO - d:POOL_HALO - d + ts, c0:c0 + cg]
        cnt = jnp.minimum(pos + 1, win).astype(F32)
        p = acc / cnt - cur
        y = _dot_bf16(p, pw_ref[gi])
        o_ref[:, c0:c0 + cg] = (y * ps_ref[:, c0:c0 + cg]).astype(o_ref.dtype)


def mix_pool(su, pool_w, pool_scale, seq, ts=256):
    m = su[0].shape[0]
    ng, cg, _ = pool_w.shape
    c = ng * cg
    kern = functools.partial(_mix_pool_kernel, tiles_per_seq=seq // ts)
    return pl.pallas_call(
        kern,
        grid=(m // ts,),
        in_specs=[_main_spec(ts, c, su[1]), _halo_spec(ts, POOL_HALO, c, su[1]),
                  pl.BlockSpec(pool_w.shape, lambda i: (0, 0, 0)),
                  pl.BlockSpec((1, c), lambda i: (0, 0))],
        out_specs=pl.BlockSpec((ts, c), lambda i: (i, 0)),
        out_shape=jax.ShapeDtypeStruct((m, c), BF16),
        scratch_shapes=[pltpu.VMEM((POOL_HALO + ts, c), F32)],
        compiler_params=_cparams(("parallel",)),
        name="mix_pool",
    )(su[0], su[0], pool_w.astype(BF16), pool_scale.reshape(1, c))


def _split3(x):
    x1 = x.astype(BF16)
    r1 = x - x1.astype(F32)
    x2 = r1.astype(BF16)
    x3 = (r1 - x2.astype(F32)).astype(BF16)
    return x1, x2, x3


INV_BASE, INV_BLOCK = 8, 16
P1_GROUP = 16


def _gdn_kernel(q_ref, k_ref, v_ref, qh_ref, kh_ref, vh_ref, z_ref, ab_ref,
                wq_ref, wk_ref, wv_ref, alog_ref, dtb_ref, nw_ref, o_ref,
                st_ref, hist, qs, ks, vs, gc_s, gcr_s, kn_s, kbq_s, vk_s, wqd_s, kd_s, u_s, a_s,
                *, n_heads):
    ts = o_ref.shape[0]
    dh = GDN_HEAD_DIM
    cc = GDN_CHUNK
    n_chunks = ts // cc
    first = pl.program_id(1) == 0

    @pl.when(first)
    def _():
        st_ref[...] = jnp.zeros_like(st_ref)

    def conv_silu(x_ref, xh_ref, w_ref, dst):
        _fill_history(hist, first, xh_ref[...], x_ref[...], SUBLANE)
        kw = w_ref.shape[0]
        acc = w_ref[kw - 1:kw, :] * x_ref[...]
        for d in range(1, kw):
            acc = acc + w_ref[kw - 1 - d:kw - d, :] * hist[SUBLANE - d:SUBLANE - d + ts, :]
        dst[...] = _silu(acc)

    conv_silu(q_ref, qh_ref, wq_ref, qs)
    conv_silu(k_ref, kh_ref, wk_ref, ks)
    conv_silu(v_ref, vh_ref, wv_ref, vs)

    ab = ab_ref[...]
    xg = ab + dtb_ref[...]
    softplus = jnp.maximum(xg, 0.0) + jnp.log1p(jnp.exp(-jnp.abs(xg)))
    g_all = -jnp.exp(alog_ref[...]) * softplus
    beta_all = _sigmoid(ab)

    rt = lax.broadcasted_iota(jnp.int32, (ts, ts), 0)
    ct = lax.broadcasted_iota(jnp.int32, (ts, ts), 1)
    tri_blk = jnp.where((rt >= ct) & ((rt // cc) == (ct // cc)), 1.0, 0.0).astype(BF16)
    g1, g2, g3 = _split3(g_all)
    gc_all = (jnp.dot(tri_blk, g1, preferred_element_type=F32)
              + jnp.dot(tri_blk, g2, preferred_element_type=F32)
              + jnp.dot(tri_blk, g3, preferred_element_type=F32))
    gc_s[...] = gc_all
    gcr_s[...] = gc_all.T

    scale = dh ** -0.5

    for h in range(n_heads):
        lanes = slice(h * dh, (h + 1) * dh)
        gc_col = jnp.broadcast_to(gc_all[:, h:h + 1], (ts, dh))
        b_col = jnp.broadcast_to(beta_all[:, n_heads + h:n_heads + h + 1], (ts, dh))
        q_all = qs[:, lanes]
        k_all = ks[:, lanes]
        q_all = q_all * (lax.rsqrt(jnp.sum(q_all * q_all, axis=-1, keepdims=True) + NORM_EPS) * scale)
        k_all = k_all * lax.rsqrt(jnp.sum(k_all * k_all, axis=-1, keepdims=True) + NORM_EPS)
        egc_all = jnp.exp(gc_col)
        kb_all = k_all * b_col
        kn_s[:, lanes] = k_all.astype(BF16)
        vk_s[:, 2 * h * dh:(2 * h + 1) * dh] = (vs[:, lanes] * b_col).astype(BF16)
        vk_s[:, (2 * h + 1) * dh:(2 * h + 2) * dh] = (kb_all * egc_all).astype(BF16)
        qd_all = q_all * egc_all
        for c in range(n_chunks):
            rows = slice(c * cc, (c + 1) * cc)
            gc_last = gc_col[(c + 1) * cc - 1:(c + 1) * cc, :]
            kbq_s[2 * c * cc:(2 * c + 1) * cc, lanes] = kb_all[rows].astype(BF16)
            kbq_s[(2 * c + 1) * cc:(2 * c + 2) * cc, lanes] = q_all[rows].astype(BF16)
            wqd_s[(2 * c + 1) * cc:(2 * c + 2) * cc, lanes] = qd_all[rows].astype(BF16)
            kd_s[rows, lanes] = (k_all[rows] * jnp.exp(gc_last - gc_col[rows])).astype(BF16)

    row = lax.broadcasted_iota(jnp.int32, (cc, cc), 0)
    col = lax.broadcasted_iota(jnp.int32, (cc, cc), 1)
    eye = (row == col).astype(F32)
    assert cc == 4 * INV_BLOCK and INV_BLOCK == 2 * INV_BASE
    in_diag_block = (row // INV_BLOCK) == (col // INV_BLOCK)
    in_base_block = (row // INV_BASE) == (col // INV_BASE)

    def p1_stages(c, h):
        rows = slice(c * cc, (c + 1) * cc)
        lanes = slice(h * dh, (h + 1) * dh)
        s = {}

        def s0():
            s["kq"] = lax.dot_general(kbq_s[2 * c * cc:(2 * c + 2) * cc, lanes], kn_s[rows, lanes], _NT,
                                      preferred_element_type=F32)

        def s1():
            pdiff = jnp.broadcast_to(gc_s[rows, h:h + 1], (cc, cc)) - gcr_s[h:h + 1, rows]
            decay = jnp.exp(jnp.where(row >= col, pdiff, -jnp.inf))
            kq = s.pop("kq")
            m = jnp.where(row > col, kq[:cc] * decay, 0.0)
            a_s[rows, h * cc:(h + 1) * cc] = (kq[cc:] * decay).astype(BF16)
            s["d"] = jnp.where(in_base_block, m, 0.0)
            s["n1"] = jnp.where(jnp.logical_and(in_diag_block, jnp.logical_not(in_base_block)), m, 0.0).astype(BF16)
            s["n"] = jnp.where(in_diag_block, 0.0, m).astype(BF16)
            s["d1"] = s["d"].astype(BF16)
            s["d2"] = _dot_bf16(s["d1"], s["d1"])

        def s2():
            d2b = s["d2"].astype(BF16)
            s["d4"] = _dot_bf16(d2b, d2b)
            s["d3"] = _dot_bf16(s.pop("d1"), d2b)

        def s3():
            x1 = eye - s.pop("d") + s.pop("d2") - s.pop("d3")
            s["tb"] = _dot_bf16(x1, eye + s.pop("d4")).astype(BF16)

        def s4():
            s["q1"] = _dot_bf16(s["tb"], s.pop("n1"))

        def s5():
            s["td"] = _dot_bf16(eye - s.pop("q1"), s.pop("tb")).astype(BF16)

        def s6():
            s["q"] = _dot_bf16(s["td"], s.pop("n"))

        def s7():
            s["qb"] = s["q"].astype(BF16)
            s["q2"] = _dot_bf16(s["qb"], s["qb"])

        def s8():
            s["q3"] = _dot_bf16(s.pop("qb"), s["q2"])

        def s9():
            t_left = eye - s.pop("q") + s.pop("q2") - s.pop("q3")
            s["t"] = _dot_bf16(t_left, s.pop("td"))

        def s10():
            uw = _dot_bf16(s.pop("t"), vk_s[rows, 2 * h * dh:(2 * h + 2) * dh])
            u_s[rows, lanes] = uw[:, :dh]
            wqd_s[2 * c * cc:(2 * c + 1) * cc, lanes] = uw[:, dh:].astype(BF16)

        return (s0, s1, s2, s3, s4, s5, s6, s7, s8, s9, s10)

    chains = [(c, h) for c in range(n_chunks) for h in range(n_heads)]
    for g0 in range(0, len(chains), P1_GROUP):
        group = [p1_stages(c, h) for c, h in chains[g0:g0 + P1_GROUP]]
        for stage in zip(*group):
            for fn in stage:
                fn()

    for c in range(n_chunks):
        rows = slice(c * cc, (c + 1) * cc)
        states = [st_ref[h] for h in range(n_heads)]
        ws_qs = [lax.dot_general(wqd_s[2 * c * cc:(2 * c + 2) * cc, h * dh:(h + 1) * dh],
                                 states[h].astype(BF16), _NN, preferred_element_type=F32)
                 for h in range(n_heads)]
        v_new = [(u_s[rows, h * dh:(h + 1) * dh] - ws_qs[h][:cc]).astype(BF16) for h in range(n_heads)]
        o_intra = [lax.dot_general(a_s[rows, h * cc:(h + 1) * cc], v_new[h], _NN, preferred_element_type=F32)
                   for h in range(n_heads)]
        s_upd = [lax.dot_general(kd_s[rows, h * dh:(h + 1) * dh], v_new[h], _TN, preferred_element_type=F32)
                 for h in range(n_heads)]
        for h in range(n_heads):
            lanes = slice(h * dh, (h + 1) * dh)
            g_last = jnp.exp(jnp.broadcast_to(gc_s[(c + 1) * cc - 1:(c + 1) * cc, h:h + 1], (1, dh)))
            st_ref[h] = states[h] * g_last + s_upd[h]
            o = ws_qs[h][cc:] + o_intra[h]
            o = o * lax.rsqrt(jnp.mean(o * o, axis=-1, keepdims=True) + NORM_EPS) * nw_ref[...]
            o_ref[rows, lanes] = (o * _silu(z_ref[rows, lanes])).astype(o_ref.dtype)


def mix_gdn(sq, sk, sv, sz, sab, conv_w, a_log, dt_bias, norm_w, batch, seq, ts=256):
    m = sq[0].shape[0]
    c = conv_w.shape[1] // 3
    n_heads = c // GDN_HEAD_DIM
    nt = seq // ts
    r = ts // SUBLANE

    def main(colblk):
        return pl.BlockSpec((ts, c), lambda b, s: (b * nt + s, colblk))

    def halo(colblk):
        return pl.BlockSpec((SUBLANE, c), lambda b, s: (jnp.maximum((b * nt + s) * r - 1, 0), colblk))

    def wspec(j):
        return pl.BlockSpec((conv_w.shape[0], c), lambda b, s: (0, j))

    vec = pl.BlockSpec((1, LANE), lambda b, s: (0, 0))
    pad = LANE - n_heads
    alog = jnp.pad(a_log, (0, pad)).reshape(1, LANE)
    dtb = jnp.pad(dt_bias, (0, pad)).reshape(1, LANE)
    kern = functools.partial(_gdn_kernel, n_heads=n_heads)
    return pl.pallas_call(
        kern,
        grid=(batch, nt),
        in_specs=[main(sq[1]), main(sk[1]), main(sv[1]),
                  halo(sq[1]), halo(sk[1]), halo(sv[1]),
                  main(sz[1]),
                  pl.BlockSpec((ts, LANE), lambda b, s: (b * nt + s, sab[1])),
                  wspec(0), wspec(1), wspec(2), vec, vec, vec],
        out_specs=pl.BlockSpec((ts, c), lambda b, s: (b * nt + s, 0)),
        out_shape=jax.ShapeDtypeStruct((m, c), BF16),
        scratch_shapes=[pltpu.VMEM((n_heads, GDN_HEAD_DIM, GDN_HEAD_DIM), F32),
                        pltpu.VMEM((SUBLANE + ts, c), F32),
                        pltpu.VMEM((ts, c), F32),
                        pltpu.VMEM((ts, c), F32),
                        pltpu.VMEM((ts, c), F32),
                        pltpu.VMEM((ts, LANE), F32),
                        pltpu.VMEM((LANE, ts), F32),
                        pltpu.VMEM((ts, c), BF16),
                        pltpu.VMEM((2 * ts, c), BF16),
                        pltpu.VMEM((ts, 2 * c), BF16),
                        pltpu.VMEM((2 * ts, c), BF16),
                        pltpu.VMEM((ts, c), BF16),
                        pltpu.VMEM((ts, c), F32),
                        pltpu.VMEM((ts, n_heads * GDN_CHUNK), BF16)],
        compiler_params=_cparams(("arbitrary", "arbitrary")),
        name="mix_gdn",
    )(sq[0], sk[0], sv[0], sq[0], sk[0], sv[0], sz[0], sab[0],
      conv_w, conv_w, conv_w, alog, dtb, norm_w.reshape(1, LANE))


def _layer(x2, xb, ss, batch, seq, li, p, stacked, last):
    cg = p["sc_conv_w"].shape[1]
    n_heads = cg // GDN_HEAD_DIM
    n_main = 9 * cg
    n_ab = 2 * n_heads

    proj = matmul_in_proj(xb, ss, p["attn_norm_w"], stacked["w_in_t"], li, n_main, n_ab, cg, tm=IN_ROWS)

    y_a = mix_sconv((proj, 0), (proj, 1), (proj, 2), p["sc_conv_w"], seq)
    y_b = mix_conformer((proj, 3), (proj, 4), p["cf_conv_w"], p["cf_conv_b"], p["cf_ln_w"], p["cf_ln_b"], seq)
    y_c = mix_gdn((proj, 5), (proj, 6), (proj, 7), (proj, 8), (proj, (n_main + cg) // LANE),
                  p["gdn_conv_w"], p["gdn_a_log"], p["gdn_dt_bias"], p["gdn_norm_w"], batch, seq)
    y_d = mix_pool((proj, 9), p["pool_w"], p["pool_scale"], seq)

    x2, xb, ss = matmul4_residual((y_a, y_b, y_c, y_d), cast_weight(stacked["w_out"], li), x2,
                                  tm=OUT_ROWS, tn=OUT_TILE)

    wd = cast_weight(stacked["w_down"], li)
    act = matmul_swiglu(xb, ss, p["ffn_norm_w"], stacked["w_gate"], stacked["w_up"], li,
                        tm=FF_ROWS, tn=FF_TILE)
    if last:
        return matmul_residual(act, wd, x2, DOWN_ROWS, DOWN_TILE, DOWN_K_SPLITS, emit_norm_inputs=False), None, None
    return matmul_residual(act, wd, x2, DOWN_ROWS, DOWN_TILE, DOWN_K_SPLITS, emit_norm_inputs=True)


def kernel(x, attn_norm_w, w_in, sc_conv_w, cf_conv_w, cf_conv_b, cf_ln_w, cf_ln_b, gdn_conv_w,
           gdn_a_log, gdn_dt_bias, gdn_norm_w, pool_w, pool_scale, w_out, ffn_norm_w, w_gate, w_up,
           w_down, final_norm_w):
    batch, seq, d = x.shape
    small = dict(attn_norm_w=attn_norm_w, sc_conv_w=sc_conv_w, cf_conv_w=cf_conv_w,
                 cf_conv_b=cf_conv_b, cf_ln_w=cf_ln_w, cf_ln_b=cf_ln_b, gdn_conv_w=gdn_conv_w,
                 gdn_a_log=gdn_a_log, gdn_dt_bias=gdn_dt_bias, gdn_norm_w=gdn_norm_w, pool_w=pool_w,
                 pool_scale=pool_scale, ffn_norm_w=ffn_norm_w)
    stacked = dict(w_in_t=jnp.swapaxes(w_in, 1, 2), w_out=w_out, w_gate=w_gate, w_up=w_up, w_down=w_down)
    x2 = x.reshape(batch * seq, d)
    xb, ss = norm_inputs(x2)
    depth = attn_norm_w.shape[0]
    for li in range(depth):
        x2, xb, ss = _layer(x2, xb, ss, batch, seq, li, {name: val[li] for name, val in small.items()},
                            stacked, last=li == depth - 1)
    out = rmsnorm(x2, final_norm_w, x.dtype)
    return out.reshape(batch, seq, d)
```
